```python
import jax, jax.numpy as jnp
from jax import lax
import numpy as np

D_MODEL = 2048
BATCH = 4
SEQ = 2048
DEPTH = 4

HEAD_DIM = 128
EPS = 1e-6
DN_QK_HEADS = 8
DN_V_HEADS = 16
DN_DK = 128
DN_DV = 128
DN_CONV = 4
DN_CHUNK = 64
SW_HEADS = 8
SW_PATTERNS = ((128, 1), (512, 4), (2048, 16))
SW_GROUPS = 3
SW_BLOCK = 128
ROPE_THETA = 500000.0
ROPE_DIM = HEAD_DIM // 4
SC_WIDTH = 3072
SC_CONV = 3

A_Q = DN_QK_HEADS * DN_DK
A_K = DN_QK_HEADS * DN_DK
A_V = DN_V_HEADS * DN_DV
A_Z = A_V
B_Q = SW_GROUPS * SW_HEADS * HEAD_DIM
B_K = SW_HEADS * HEAD_DIM
B_V = SW_HEADS * HEAD_DIM
B_Z = B_V
HYB_IN = A_Q + A_K + A_V + A_Z + 2 * DN_V_HEADS + B_Q + B_K + B_V + B_Z
HYB_MIX = A_V + B_V
N_EVEN = (DEPTH + 1) // 2
N_ODD = DEPTH // 2

kernel_name = "hybrid_deltanet_dilated_shortconv"


def rmsnorm(x, w):
    xf = x.astype(jnp.float32)
    y = xf * lax.rsqrt(jnp.mean(xf * xf, axis=-1, keepdims=True) + EPS)
    return (y * w.astype(jnp.float32)).astype(x.dtype)


def causal_depthwise_conv(x, w):
    k = w.shape[1]
    rhs = jnp.transpose(w)[:, None, :].astype(x.dtype)
    return lax.conv_general_dilated(x, rhs, window_strides=(1,), padding=[(k - 1, 0)],
                                    dimension_numbers=('NWC', 'WIO', 'NWC'),
                                    feature_group_count=x.shape[-1])


def partial_rope(x, pos):
    half = ROPE_DIM // 2
    inv = jnp.power(jnp.float32(ROPE_THETA), -jnp.arange(half, dtype=jnp.float32) * 2.0 / ROPE_DIM)
    ang = pos.astype(jnp.float32)[:, None] * inv[None, :]
    cos = jnp.cos(ang)[None, :, None, :]
    sin = jnp.sin(ang)[None, :, None, :]
    x1 = x[..., :half]
    x2 = x[..., half:ROPE_DIM]
    return jnp.concatenate([x1 * cos - x2 * sin, x2 * cos + x1 * sin, x[..., ROPE_DIM:]], axis=-1)


def l2norm(x):
    return x * lax.rsqrt(jnp.sum(x * x, axis=-1, keepdims=True) + EPS)


def gated_delta_chunked(q, k, v, beta, g):
    b, s, h, dk = q.shape
    dv = v.shape[-1]
    n = s // DN_CHUNK
    c = DN_CHUNK

    def chunks(t):
        return t.reshape(b, n, c, h, t.shape[-1]).transpose(0, 1, 3, 2, 4)

    q, k, v = chunks(q), chunks(k), chunks(v)
    beta = beta.reshape(b, n, c, h).transpose(0, 1, 3, 2)
    g = jnp.cumsum(g.reshape(b, n, c, h).transpose(0, 1, 3, 2), axis=-1)
    idx = jnp.arange(c)
    causal = idx[:, None] >= idx[None, :]
    strict = idx[:, None] > idx[None, :]
    diff = g[..., :, None] - g[..., None, :]
    decay = jnp.exp(jnp.where(causal, diff, -jnp.inf))
    kk = jnp.einsum('bnhik,bnhjk->bnhij', k, k)
    a_strict = jnp.where(strict, beta[..., :, None] * kk * decay, 0.0)
    rhs = jnp.concatenate([v * beta[..., None], k * (beta * jnp.exp(g))[..., None]], axis=-1)
    sol = lax.linalg.triangular_solve(a_strict, rhs, left_side=True, lower=True, unit_diagonal=True)
    u = sol[..., :dv]
    w = sol[..., dv:]
    qk = jnp.einsum('bnhik,bnhjk->bnhij', q, k) * decay

    def step(state, inp):
        q_c, k_c, u_c, w_c, g_c, qk_c = inp
        v_new = u_c - jnp.einsum('bhck,bhkv->bhcv', w_c, state)
        o = jnp.einsum('bhck,bhkv->bhcv', q_c * jnp.exp(g_c)[..., None], state) + \
            jnp.einsum('bhij,bhjv->bhiv', qk_c, v_new)
        g_last = g_c[..., -1]
        state = state * jnp.exp(g_last)[..., None, None] + jnp.einsum(
            'bhck,bhcv->bhkv', k_c * jnp.exp(g_last[..., None] - g_c)[..., None], v_new)
        return state, o

    xs = tuple(jnp.moveaxis(t, 1, 0) for t in (q, k, u, w, g, qk))
    state0 = jnp.zeros((b, h, dk, dv), jnp.float32)
    _, o = lax.scan(step, state0, xs)
    return o.transpose(1, 0, 3, 2, 4).reshape(b, s, h, dv)


def dilated_window_group(q, k, v, dil):
    b, s, h, dh = q.shape
    wb = SW_BLOCK
    ln = s // dil
    nb = -(-ln // wb)
    lp = nb * wb

    def strided(t):
        t = t.reshape(b, ln, dil, h, dh).transpose(0, 2, 3, 1, 4)
        t = jnp.pad(t, ((0, 0), (0, 0), (0, 0), (0, lp - ln), (0, 0)))
        return t.reshape(b, dil, h, nb, wb, dh)

    def with_prev(t):
        prev = jnp.concatenate([jnp.zeros_like(t[:, :, :, :1]), t[:, :, :, :-1]], axis=3)
        return jnp.concatenate([prev, t], axis=4)

    qs = strided(q)
    kb = with_prev(strided(k))
    vb = with_prev(strided(v))
    sc = jnp.einsum('brhnqc,brhnkc->brhnqk', qs, kb) * (dh ** -0.5)
    i = jnp.arange(wb)[:, None]
    j = jnp.arange(2 * wb)[None, :]
    dist = i + wb - j
    blk = jnp.arange(nb)[:, None, None]
    valid = (dist >= 0) & (dist <= wb) & (blk * wb - wb + j >= 0)
    sc = jnp.where(valid, sc, -jnp.inf)
    m = jnp.max(sc, axis=-1, keepdims=True)
    p = jnp.exp(sc - m)
    den = jnp.sum(p, axis=-1)
    num = jnp.einsum('brhnqk,brhnkc->brhnqc', p, vb)

    def unstride(t):
        rest = t.shape[5:]
        t = t.reshape((b, dil, h, lp) + rest)[:, :, :, :ln]
        return jnp.moveaxis(t, 3, 1).reshape((b, s, h) + rest)

    return unstride(num), unstride(den), unstride(m[..., 0])


def dilated_attention(q, k, v):
    nums, dens, maxs = [], [], []
    for gi, (window, dil) in enumerate(SW_PATTERNS):
        n_, d_, m_ = dilated_window_group(q[:, :, gi], k, v, dil)
        nums.append(n_)
        dens.append(d_)
        maxs.append(m_)
    mx = jnp.max(jnp.stack(maxs), axis=0)
    wts = [jnp.exp(m_ - mx) for m_ in maxs]
    num = sum(w_[..., None] * n_ for w_, n_ in zip(wts, nums))
    den = sum(w_ * d_ for w_, d_ in zip(wts, dens))
    return num / den[..., None]


def hybrid_layer(h, w_in, conv_w, a_log, dt_bias, dn_norm_w, w_out, pos):
    b, s, _ = h.shape
    f32 = jnp.float32
    proj = h @ w_in
    sizes = [A_Q, A_K, A_V, A_Z, DN_V_HEADS, DN_V_HEADS, B_Q, B_K, B_V, B_Z]
    offs = []
    acc = 0
    for sz in sizes[:-1]:
        acc += sz
        offs.append(acc)
    aq, ak, av, az, ab, aa, bq, bk, bv, bz = jnp.split(proj, offs, axis=-1)

    qkv = jax.nn.silu(causal_depthwise_conv(jnp.concatenate([aq, ak, av], axis=-1), conv_w)).astype(f32)
    q_a = l2norm(qkv[..., :A_Q].reshape(b, s, DN_QK_HEADS, DN_DK)) * (DN_DK ** -0.5)
    k_a = l2norm(qkv[..., A_Q:A_Q + A_K].reshape(b, s, DN_QK_HEADS, DN_DK))
    v_a = qkv[..., A_Q + A_K:].reshape(b, s, DN_V_HEADS, DN_DV)
    rep = DN_V_HEADS // DN_QK_HEADS
    q_a = jnp.repeat(q_a, rep, axis=2)
    k_a = jnp.repeat(k_a, rep, axis=2)
    beta = jax.nn.sigmoid(ab.astype(f32))
    g = -jnp.exp(a_log.astype(f32)) * jax.nn.softplus(aa.astype(f32) + dt_bias.astype(f32))
    o_a = gated_delta_chunked(q_a, k_a, v_a, beta, g)
    o_a = o_a * lax.rsqrt(jnp.mean(o_a * o_a, axis=-1, keepdims=True) + EPS) * dn_norm_w.astype(f32)
    y_a = (o_a.reshape(b, s, A_V) * jax.nn.silu(az.astype(f32))).astype(h.dtype)

    q_b = partial_rope(bq.astype(f32).reshape(b, s, SW_GROUPS * SW_HEADS, HEAD_DIM), pos)
    q_b = q_b.reshape(b, s, SW_GROUPS, SW_HEADS, HEAD_DIM)
    k_b = partial_rope(bk.astype(f32).reshape(b, s, SW_HEADS, HEAD_DIM), pos)
    v_b = bv.astype(f32).reshape(b, s, SW_HEADS, HEAD_DIM)
    o_b = dilated_attention(q_b, k_b, v_b)
    y_b = (o_b.reshape(b, s, B_V) * jax.nn.silu(bz.astype(f32))).astype(h.dtype)

    return jnp.concatenate([y_a, y_b], axis=-1) @ w_out


def shortconv_layer(h, w_in, conv_w, w_out):
    gate_b, gate_c, u, z = jnp.split(h @ w_in, 4, axis=-1)
    y = gate_b * causal_depthwise_conv(gate_c * u, conv_w)
    return (y * jax.nn.silu(z)) @ w_out


def setup_inputs(seed: int = 0) -> dict:
    key = jax.random.key(seed)
    ks = jax.random.split(key, 14)
    f32 = jnp.float32

    def nrm(k, shape, scale):
        return jax.random.normal(k, shape, f32) * scale

    x = nrm(ks[0], (BATCH, SEQ, D_MODEL), 1.0)
    norm_w = 1.0 + nrm(ks[1], (DEPTH, D_MODEL), 0.05)
    hyb_w_in = nrm(ks[2], (N_EVEN, D_MODEL, HYB_IN), D_MODEL ** -0.5)
    dn_conv_w = nrm(ks[3], (N_EVEN, A_Q + A_K + A_V, DN_CONV), DN_CONV ** -0.5)
    dn_a_log = jnp.log(jax.random.uniform(ks[4], (N_EVEN, DN_V_HEADS), f32, 1.0, 16.0))
    dt = jnp.exp(jax.random.uniform(ks[5], (N_EVEN, DN_V_HEADS), f32, jnp.log(1e-3), jnp.log(1e-1)))
    dn_dt_bias = dt + jnp.log(-jnp.expm1(-dt))
    dn_norm_w = 1.0 + nrm(ks[6], (N_EVEN, DN_DV), 0.05)
    hyb_w_out = nrm(ks[7], (N_EVEN, HYB_MIX, D_MODEL), HYB_MIX ** -0.5)
    sc_w_in = nrm(ks[8], (N_ODD, D_MODEL, 4 * SC_WIDTH), D_MODEL ** -0.5)
    sc_conv_w = nrm(ks[9], (N_ODD, SC_WIDTH, SC_CONV), SC_CONV ** -0.5)
    sc_w_out = nrm(ks[10], (N_ODD, SC_WIDTH, D_MODEL), SC_WIDTH ** -0.5)
    final_norm_w = 1.0 + nrm(ks[11], (D_MODEL,), 0.05)
    return {"x": x, "norm_w": norm_w, "hyb_w_in": hyb_w_in, "dn_conv_w": dn_conv_w,
            "dn_a_log": dn_a_log, "dn_dt_bias": dn_dt_bias, "dn_norm_w": dn_norm_w,
            "hyb_w_out": hyb_w_out, "sc_w_in": sc_w_in, "sc_conv_w": sc_conv_w,
            "sc_w_out": sc_w_out, "final_norm_w": final_norm_w}


def reference(x, norm_w, hyb_w_in, dn_conv_w, dn_a_log, dn_dt_bias, dn_norm_w, hyb_w_out,
              sc_w_in, sc_conv_w, sc_w_out, final_norm_w):
    pos = jnp.arange(x.shape[1], dtype=jnp.int32)
    h = x
    for layer in range(DEPTH):
        hn = rmsnorm(h, norm_w[layer])
        li = layer // 2
        if layer % 2 == 0:
            h = h + hybrid_layer(hn, hyb_w_in[li], dn_conv_w[li], dn_a_log[li], dn_dt_bias[li],
                                 dn_norm_w[li], hyb_w_out[li], pos)
        else:
            h = h + shortconv_layer(hn, sc_w_in[li], sc_conv_w[li], sc_w_out[li])
    return rmsnorm(h, final_norm_w)
```

```python
import functools

import jax
import jax.numpy as jnp
from jax import lax
from jax.experimental import pallas as pl
from jax.experimental.pallas import tpu as pltpu

F32 = jnp.float32
BF16 = jnp.bfloat16

D_MODEL = 2048
DEPTH = 4
HEAD_DIM = 128
EPS = 1e-6
DN_QK_HEADS = 8
DN_V_HEADS = 16
DN_CONV = 4
DN_CHUNK = 64
SW_HEADS = 8
SW_DILATIONS = (1, 4, 16)
SW_BLOCK = 128
ROPE_THETA = 500000.0
ROPE_DIM = HEAD_DIM // 4
ROPE_HALF = ROPE_DIM // 2
SC_WIDTH = 3072
SC_CONV = 3

A_Q = DN_QK_HEADS * HEAD_DIM
A_V = DN_V_HEADS * HEAD_DIM
A_WIDTH = 2 * A_Q + 2 * A_V
GATE_OFF = A_WIDTH
B_OFF = A_WIDTH + 2 * DN_V_HEADS
B_WIDTH = (len(SW_DILATIONS) + 3) * SW_HEADS * HEAD_DIM
HYB_MIX = A_V + SW_HEADS * HEAD_DIM

VMEM_LIMIT = 56 * 1024 * 1024
NEG_BIG = -1e30

_NT = (((1,), (1,)), ((), ()))
_TN = (((0,), (0,)), ((), ()))


def _params(*sem):
    return pltpu.CompilerParams(dimension_semantics=sem, vmem_limit_bytes=VMEM_LIMIT)


def _rmsnorm_kernel(x_ref, w_ref, o_ref):
    x = x_ref[...]
    ms = jnp.mean(x * x, axis=-1, keepdims=True)
    o_ref[...] = (x * lax.rsqrt(ms + EPS) * w_ref[...]).astype(o_ref.dtype)


def _rmsnorm(x, w, out_dtype, tm=512):
    m, d = x.shape
    return pl.pallas_call(
        _rmsnorm_kernel,
        grid=(m // tm,),
        in_specs=[pl.BlockSpec((tm, d), lambda i: (i, 0)),
                  pl.BlockSpec((1, d), lambda i: (0, 0))],
        out_specs=pl.BlockSpec((tm, d), lambda i: (i, 0)),
        out_shape=jax.ShapeDtypeStruct((m, d), out_dtype),
        compiler_params=_params("arbitrary"),
        name="rmsnorm",
    )(x, w.reshape(1, d))


def _mm_kernel(x_ref, w_ref, o_ref, wb_ref):
    @pl.when(pl.program_id(1) == 0)
    def _():
        wb_ref[...] = w_ref[...].astype(BF16)

    o_ref[...] = jnp.dot(x_ref[...], wb_ref[...], preferred_element_type=F32).astype(o_ref.dtype)


def _matmul(x, w, layer, n_cols, out_dtype, tm=1024, tn=1024):
    m, k = x.shape
    return pl.pallas_call(
        _mm_kernel,
        grid=(n_cols // tn, m // tm),
        in_specs=[pl.BlockSpec((tm, k), lambda j, i: (i, 0)),
                  pl.BlockSpec((None, k, tn), lambda j, i: (layer, 0, j))],
        out_specs=pl.BlockSpec((tm, tn), lambda j, i: (i, j)),
        out_shape=jax.ShapeDtypeStruct((m, n_cols), out_dtype),
        scratch_shapes=[pltpu.VMEM((k, tn), BF16)],
        compiler_params=_params("arbitrary", "arbitrary"),
        name="in_proj",
    )(x, w)


def _mm_res_kernel(*refs, k_sizes):
    nx = len(k_sizes)
    x_refs = refs[:nx]
    w_ref, h_ref, o_ref, wb_ref = refs[nx:]

    @pl.when(pl.program_id(1) == 0)
    def _():
        wb_ref[...] = w_ref[...].astype(BF16)

    acc = h_ref[...]
    off = 0
    for x_ref, ks in zip(x_refs, k_sizes):
        acc = acc + jnp.dot(x_ref[...], wb_ref[off:off + ks, :], preferred_element_type=F32)
        off += ks
    o_ref[...] = acc


def _out_proj_residual(xs, w, layer, h, tm=512, tn=512):
    m, n = h.shape
    k_sizes = tuple(x.shape[1] for x in xs)
    k = sum(k_sizes)
    return pl.pallas_call(
        functools.partial(_mm_res_kernel, k_sizes=k_sizes),
        grid=(n // tn, m // tm),
        in_specs=[pl.BlockSpec((tm, ks), lambda j, i: (i, 0)) for ks in k_sizes]
        + [pl.BlockSpec((None, k, tn), lambda j, i: (layer, 0, j)),
           pl.BlockSpec((tm, tn), lambda j, i: (i, j))],
        out_specs=pl.BlockSpec((tm, tn), lambda j, i: (i, j)),
        out_shape=jax.ShapeDtypeStruct((m, n), F32),
        scratch_shapes=[pltpu.VMEM((k, tn), BF16)],
        compiler_params=_params("arbitrary", "arbitrary"),
        name="out_proj",
    )(*xs, w, h)


def _softplus(x):
    return jnp.maximum(x, 0.0) + jnp.log1p(jnp.exp(-jnp.abs(x)))


def _gates_kernel(x_ref, w_ref, wt_ref, alog_ref, dtb_ref, alog_c_ref, dtb_c_ref,
                  col_ref, row_ref, *, tm):
    nh = DN_V_HEADS
    c = DN_CHUNK
    x = x_ref[...]
    p = jnp.dot(x, w_ref[...], preferred_element_type=F32)
    beta = jax.nn.sigmoid(p[:, :nh])
    g = -jnp.exp(alog_ref[...]) * _softplus(p[:, nh:] + dtb_ref[...])
    ri = lax.broadcasted_iota(jnp.int32, (tm, tm), 0)
    ci = lax.broadcasted_iota(jnp.int32, (tm, tm), 1)
    same = (ri // c) == (ci // c)
    lower = jnp.where(same & (ci <= ri), 1.0, 0.0).astype(F32)
    gc = jnp.dot(lower, g, preferred_element_type=F32, precision=lax.Precision.HIGHEST)
    for hq in range(DN_QK_HEADS):
        col_ref[hq, :, 0:2] = beta[:, 2 * hq:2 * hq + 2]
        col_ref[hq, :, 2:4] = gc[:, 2 * hq:2 * hq + 2]
    pt = lax.dot_general(wt_ref[...], x, _NT, preferred_element_type=F32)
    gt = -jnp.exp(alog_c_ref[...]) * _softplus(pt[nh:, :] + dtb_c_ref[...])
    upper = jnp.where(same & (ri <= ci), 1.0, 0.0).astype(F32)
    gct = jnp.dot(gt, upper, preferred_element_type=F32, precision=lax.Precision.HIGHEST)
    for ch in range(tm // c):
        row_ref[ch] = gct[:, ch * c:(ch + 1) * c]


def _gates(hn, w_small, a_log, dt_bias, batch, seq, tm=512):
    m, k = hn.shape
    nh = DN_V_HEADS
    nch = seq // DN_CHUNK
    wb = w_small.astype(BF16)
    steps = seq // tm
    col, row = pl.pallas_call(
        functools.partial(_gates_kernel, tm=tm),
        grid=(m // tm,),
        in_specs=[pl.BlockSpec((tm, k), lambda i: (i, 0)),
                  pl.BlockSpec((k, 2 * nh), lambda i: (0, 0)),
                  pl.BlockSpec((2 * nh, k), lambda i: (0, 0)),
                  pl.BlockSpec((1, nh), lambda i: (0, 0)),
                  pl.BlockSpec((1, nh), lambda i: (0, 0)),
                  pl.BlockSpec((nh, 1), lambda i: (0, 0)),
                  pl.BlockSpec((nh, 1), lambda i: (0, 0))],
        out_specs=[pl.BlockSpec((None, DN_QK_HEADS, tm, 4), lambda i: (i // steps, 0, i % steps, 0)),
                   pl.BlockSpec((None, tm // DN_CHUNK, nh, DN_CHUNK), lambda i: (i // steps, i % steps, 0, 0))],
        out_shape=[jax.ShapeDtypeStruct((batch, DN_QK_HEADS, seq, 4), F32),
                   jax.ShapeDtypeStruct((batch, nch, nh, DN_CHUNK), F32)],
        compiler_params=_params("arbitrary"),
        name="dn_gates",
    )(hn, wb, wb.T, a_log.reshape(1, nh), dt_bias.reshape(1, nh),
      a_log.reshape(nh, 1), dt_bias.reshape(nh, 1))
    return col, row


def _conv_silu(ref, cw_ref, n, t0, width):
    c = DN_CHUNK
    cur = ref[pl.ds(t0, c), :]
    pstart = pl.multiple_of(jnp.maximum(t0 - 8, 0), 8)
    prev = jnp.where(n > 0, ref[pl.ds(pstart, 8), :], 0.0)
    xx = jnp.concatenate([prev, cur], axis=0)
    acc = cw_ref[width - 1:width, :] * cur
    for i in range(width - 1):
        lo = 8 - (width - 1) + i
        acc = acc + cw_ref[i:i + 1, :] * xx[lo:lo + c, :]
    return acc * jax.nn.sigmoid(acc)


def _l2norm(x):
    return x * lax.rsqrt(jnp.sum(x * x, axis=-1, keepdims=True) + EPS)


def _unit_lower_inverse(a):
    c = a.shape[0]
    hi = lax.Precision.HIGHEST
    eye = (lax.broadcasted_iota(jnp.int32, (c, c), 0) == lax.broadcasted_iota(jnp.int32, (c, c), 1)).astype(F32)
    p = -a
    t = eye + p
    span = 2
    while span < c:
        p = jnp.dot(p, p, preferred_element_type=F32, precision=hi)
        t = t + jnp.dot(t, p, preferred_element_type=F32, precision=hi)
        span *= 2
    return t


def _deltanet_kernel(q_ref, k_ref, v_ref, z_ref, cwq_ref, cwk_ref, cwv_ref, gcol_ref, grow_ref, nw_ref,
                     o_ref, state_ref, *, seq):
    c = DN_CHUNK
    dk = HEAD_DIM
    hq = pl.program_id(1)
    state_ref[...] = jnp.zeros_like(state_ref)
    ri = lax.broadcasted_iota(jnp.int32, (c, c), 0)
    ci = lax.broadcasted_iota(jnp.int32, (c, c), 1)
    causal = ri >= ci
    strict = ri > ci

    def body(n, carry):
        t0 = pl.multiple_of(n * c, c)
        q = _l2norm(_conv_silu(q_ref, cwq_ref, n, t0, DN_CONV)) * (dk ** -0.5)
        k = _l2norm(_conv_silu(k_ref, cwk_ref, n, t0, DN_CONV))
        v2 = _conv_silu(v_ref, cwv_ref, n, t0, DN_CONV)
        z2 = z_ref[pl.ds(t0, c), :]
        qb = q.astype(BF16)
        kb = k.astype(BF16)
        kk = lax.dot_general(kb, kb, _NT, preferred_element_type=F32)
        qk = lax.dot_general(qb, kb, _NT, preferred_element_type=F32)
        gcols = gcol_ref[pl.ds(t0, c), :]
        for j in range(2):
            beta = gcols[:, j:j + 1]
            gc = gcols[:, 2 + j:3 + j]
            gr = grow_ref[n, pl.ds(2 * hq + j, 1), :]
            decay = jnp.exp(jnp.where(causal, gc - gr, NEG_BIG))
            a = jnp.where(strict, beta * kk * decay, 0.0)
            t = _unit_lower_inverse(a).astype(BF16)
            eg = jnp.exp(gc)
            v = v2[:, j * dk:(j + 1) * dk]
            u = jnp.dot(t, (v * beta).astype(BF16), preferred_element_type=F32)
            w = jnp.dot(t, (k * (beta * eg)).astype(BF16), preferred_element_type=F32)
            s = state_ref[j]
            sb = s.astype(BF16)
            v_new = u - jnp.dot(w.astype(BF16), sb, preferred_element_type=F32)
            vnb = v_new.astype(BF16)
            o = jnp.dot((q * eg).astype(BF16), sb, preferred_element_type=F32) + jnp.dot(
                (qk * decay).astype(BF16), vnb, preferred_element_type=F32)
            g_last = gc[c - 1:c, :]
            k_dec = (k * jnp.exp(g_last - gc)).astype(BF16)
            state_ref[j] = s * jnp.exp(g_last) + lax.dot_general(k_dec, vnb, _TN, preferred_element_type=F32)
            o = o * lax.rsqrt(jnp.mean(o * o, axis=-1, keepdims=True) + EPS) * nw_ref[...]
            z = z2[:, j * dk:(j + 1) * dk]
            o_ref[pl.ds(t0, c), j * dk:(j + 1) * dk] = (o * (z * jax.nn.sigmoid(z))).astype(o_ref.dtype)
        return carry

    lax.fori_loop(0, seq // c, body, 0)


def _deltanet(proj_a, conv_w_t, gcol, grow, norm_w, batch, seq):
    dk = HEAD_DIM
    nq = DN_QK_HEADS
    nch = seq // DN_CHUNK
    return pl.pallas_call(
        functools.partial(_deltanet_kernel, seq=seq),
        grid=(batch, nq),
        in_specs=[pl.BlockSpec((seq, dk), lambda b, h: (b, h)),
                  pl.BlockSpec((seq, dk), lambda b, h: (b, nq + h)),
                  pl.BlockSpec((seq, 2 * dk), lambda b, h: (b, nq + h)),
                  pl.BlockSpec((seq, 2 * dk), lambda b, h: (b, 2 * nq + h)),
                  pl.BlockSpec((DN_CONV, dk), lambda b, h: (0, h)),
                  pl.BlockSpec((DN_CONV, dk), lambda b, h: (0, nq + h)),
                  pl.BlockSpec((DN_CONV, 2 * dk), lambda b, h: (0, nq + h)),
                  pl.BlockSpec((None, None, seq, 4), lambda b, h: (b, h, 0, 0)),
                  pl.BlockSpec((None, nch, DN_V_HEADS, DN_CHUNK), lambda b, h: (b, 0, 0, 0)),
                  pl.BlockSpec((1, dk), lambda b, h: (0, 0))],
        out_specs=pl.BlockSpec((seq, 2 * dk), lambda b, h: (b, h)),
        out_shape=jax.ShapeDtypeStruct((batch * seq, A_V), BF16),
        scratch_shapes=[pltpu.VMEM((2, dk, dk), F32)],
        compiler_params=_params("arbitrary", "arbitrary"),
        name="deltanet",
    )(proj_a, proj_a, proj_a, proj_a, conv_w_t, conv_w_t, conv_w_t, gcol, grow, norm_w.reshape(1, dk))


def _rope_table_kernel(cos_ref, sin_ref):
    shape = cos_ref.shape
    pos = lax.broadcasted_iota(jnp.int32, shape, 0).astype(F32)
    lane = lax.broadcasted_iota(jnp.int32, shape, 1)
    idx = (lane % ROPE_HALF).astype(F32)
    inv = jnp.exp(idx * (-2.0 / ROPE_DIM) * jnp.log(jnp.float32(ROPE_THETA)))
    ang = pos * inv
    cos = jnp.cos(ang)
    sin = jnp.sin(ang)
    cos_ref[...] = jnp.where(lane < ROPE_DIM, cos, 1.0)
    sin_ref[...] = jnp.where(lane < ROPE_HALF, -sin, jnp.where(lane < ROPE_DIM, sin, 0.0))


def _rope_tables(seq):
    shp = jax.ShapeDtypeStruct((seq, HEAD_DIM), F32)
    return pl.pallas_call(_rope_table_kernel, out_shape=[shp, shp], name="rope_tables")()


def _attn_kernel(q0_ref, q1_ref, q2_ref, k_ref, v_ref, z_ref, cos_ref, sin_ref, o_ref,
                 qs_ref, ks_ref, m_ref, l_ref, acc_ref, *, seq):
    wb = SW_BLOCK
    dh = HEAD_DIM
    rows = 256
    lane = lax.broadcasted_iota(jnp.int32, (rows, dh), 1)

    def rope(x, cos, sin):
        swapped = jnp.where(lane < ROPE_HALF, pltpu.roll(x, dh - ROPE_HALF, 1), pltpu.roll(x, ROPE_HALF, 1))
        return x * cos + swapped * sin

    def rope_body(i, carry):
        r0 = pl.multiple_of(i * rows, rows)
        cos = cos_ref[pl.ds(r0, rows), :]
        sin = sin_ref[pl.ds(r0, rows), :]
        for g, q_ref in enumerate((q0_ref, q1_ref, q2_ref)):
            qs_ref[g, pl.ds(r0, rows), :] = rope(q_ref[pl.ds(r0, rows), :], cos, sin) * (dh ** -0.5)
        ks_ref[pl.ds(r0, rows), :] = rope(k_ref[pl.ds(r0, rows), :], cos, sin)
        return carry

    lax.fori_loop(0, seq // rows, rope_body, 0)

    qi = lax.broadcasted_iota(jnp.int32, (wb, wb), 0)
    kj = lax.broadcasted_iota(jnp.int32, (wb, wb), 1)
    cur_ok = kj <= qi
    prev_ok = kj >= qi

    for g, dil in enumerate(SW_DILATIONS):
        nb = max(seq // dil // wb, 1)
        has_prev = nb > 1

        def block_body(idx, carry, g=g, dil=dil, nb=nb, has_prev=has_prev):
            r = idx // nb
            n = idx % nb
            start = r + n * (dil * wb)

            def rows_of(ref, s):
                if dil == 1:
                    return ref[pl.ds(s, wb), :]
                return ref[pl.ds(s, wb, stride=dil), :]

            q = rows_of(qs_ref.at[g], start).astype(BF16)
            k_cur = rows_of(ks_ref, start).astype(BF16)
            v_cur = rows_of(v_ref, start).astype(BF16)
            s_cur = jnp.where(cur_ok, lax.dot_general(q, k_cur, _NT, preferred_element_type=F32), NEG_BIG)
            m_blk = jnp.max(s_cur, axis=-1, keepdims=True)
            if has_prev:
                pstart = jnp.maximum(start - dil * wb, 0)
                k_prev = rows_of(ks_ref, pstart).astype(BF16)
                v_prev = rows_of(v_ref, pstart).astype(BF16)
                ok = prev_ok & (n > 0)
                s_prev = jnp.where(ok, lax.dot_general(q, k_prev, _NT, preferred_element_type=F32), NEG_BIG)
                m_blk = jnp.maximum(m_blk, jnp.max(s_prev, axis=-1, keepdims=True))
            p_cur = jnp.exp(s_cur - m_blk)
            l_blk = jnp.sum(p_cur, axis=-1, keepdims=True)
            acc_blk = jnp.dot(p_cur.astype(BF16), v_cur, preferred_element_type=F32)
            if has_prev:
                p_prev = jnp.exp(s_prev - m_blk)
                l_blk = l_blk + jnp.sum(p_prev, axis=-1, keepdims=True)
                acc_blk = acc_blk + jnp.dot(p_prev.astype(BF16), v_prev, preferred_element_type=F32)
            m_blk = jnp.broadcast_to(m_blk, (wb, dh))
            l_blk = jnp.broadcast_to(l_blk, (wb, dh))
            if g == 0:
                m_ref[pl.ds(start, wb), :] = m_blk
                l_ref[pl.ds(start, wb), :] = l_blk
                acc_ref[pl.ds(start, wb), :] = acc_blk
            else:
                sl = pl.ds(start, wb, stride=dil)
                m_old = m_ref[sl, :]
                m_new = jnp.maximum(m_old, m_blk)
                a_old = jnp.exp(m_old - m_new)
                a_blk = jnp.exp(m_blk - m_new)
                m_ref[sl, :] = m_new
                l_ref[sl, :] = a_old * l_ref[sl, :] + a_blk * l_blk
                acc_ref[sl, :] = a_old * acc_ref[sl, :] + a_blk * acc_blk
            return carry

        lax.fori_loop(0, dil * nb, block_body, 0)

    def out_body(i, carry):
        r0 = pl.multiple_of(i * rows, rows)
        sl = pl.ds(r0, rows)
        z = z_ref[sl, :]
        o = acc_ref[sl, :] / l_ref[sl, :]
        o_ref[sl, :] = (o * (z * jax.nn.sigmoid(z))).astype(o_ref.dtype)
        return carry

    lax.fori_loop(0, seq // rows, out_body, 0)


def _dilated_attention(proj_b, cos_tab, sin_tab, batch, seq):
    dh = HEAD_DIM
    nh = SW_HEADS
    ng = len(SW_DILATIONS)

    def col(block):
        return pl.BlockSpec((seq, dh), lambda b, h: (b, block * nh + h))

    tab = pl.BlockSpec((seq, dh), lambda b, h: (0, 0))
    return pl.pallas_call(
        functools.partial(_attn_kernel, seq=seq),
        grid=(batch, nh),
        in_specs=[col(0), col(1), col(2), col(ng), col(ng + 1), col(ng + 2), tab, tab],
        out_specs=pl.BlockSpec((seq, dh), lambda b, h: (b, h)),
        out_shape=jax.ShapeDtypeStruct((batch * seq, nh * dh), BF16),
        scratch_shapes=[pltpu.VMEM((ng, seq, dh), F32), pltpu.VMEM((seq, dh), F32),
                        pltpu.VMEM((seq, dh), F32), pltpu.VMEM((seq, dh), F32), pltpu.VMEM((seq, dh), F32)],
        compiler_params=_params("arbitrary", "arbitrary"),
        name="dilated_attn",
    )(proj_b, proj_b, proj_b, proj_b, proj_b, proj_b, cos_tab, sin_tab)


def _shortconv_kernel(x_ref, wb_ref, wc_ref, wu_ref, wz_ref, cw_ref, o_ref, wq_ref, halo_ref, *, steps_per_seq):
    i = pl.program_id(1)

    @pl.when(i == 0)
    def _():
        for g, w_ref in enumerate((wb_ref, wc_ref, wu_ref, wz_ref)):
            wq_ref[g] = w_ref[...].astype(BF16)

    x = x_ref[...]
    tm = x.shape[0]
    gate_c = jnp.dot(x, wq_ref[1], preferred_element_type=F32)
    u = jnp.dot(x, wq_ref[2], preferred_element_type=F32)
    cu = gate_c * u
    prev = jnp.where(i % steps_per_seq != 0, halo_ref[...], 0.0)
    halo_ref[...] = cu[tm - 8:, :]
    xx = jnp.concatenate([prev, cu], axis=0)
    conv = cw_ref[SC_CONV - 1:SC_CONV, :] * cu
    for t in range(SC_CONV - 1):
        lo = 8 - (SC_CONV - 1) + t
        conv = conv + cw_ref[t:t + 1, :] * xx[lo:lo + tm, :]
    gate_b = jnp.dot(x, wq_ref[0], preferred_element_type=F32)
    z = jnp.dot(x, wq_ref[3], preferred_element_type=F32)
    o_ref[...] = (gate_b * conv * (z * jax.nn.sigmoid(z))).astype(o_ref.dtype)


def _shortconv_mix(hn, w_in, layer, conv_w_t, seq, tm=512, tn=256):
    m, k = hn.shape
    nblk = SC_WIDTH // tn

    def wspec(g):
        return pl.BlockSpec((None, k, tn), lambda j, i: (layer, 0, g * nblk + j))

    return pl.pallas_call(
        functools.partial(_shortconv_kernel, steps_per_seq=seq // tm),
        grid=(nblk, m // tm),
        in_specs=[pl.BlockSpec((tm, k), lambda j, i: (i, 0)), wspec(0), wspec(1), wspec(2), wspec(3),
                  pl.BlockSpec((None, SC_CONV, tn), lambda j, i: (layer, 0, j))],
        out_specs=pl.BlockSpec((tm, tn), lambda j, i: (i, j)),
        out_shape=jax.ShapeDtypeStruct((m, SC_WIDTH), BF16),
        scratch_shapes=[pltpu.VMEM((4, k, tn), BF16), pltpu.VMEM((8, tn), F32)],
        compiler_params=_params("arbitrary", "arbitrary"),
        name="shortconv_mix",
    )(hn, w_in, w_in, w_in, w_in, conv_w_t)


def kernel(x, norm_w, hyb_w_in, dn_conv_w, dn_a_log, dn_dt_bias, dn_norm_w, hyb_w_out, sc_w_in, sc_conv_w,
           sc_w_out, final_norm_w):
    batch, seq, d = x.shape
    h = x.reshape(batch * seq, d)
    cos_tab, sin_tab = _rope_tables(seq)
    dn_conv_t = jnp.swapaxes(dn_conv_w, 1, 2)
    sc_conv_t = jnp.swapaxes(sc_conv_w, 1, 2)
    for layer in range(DEPTH):
        hn = _rmsnorm(h, norm_w[layer], BF16)
        li = layer // 2
        if layer % 2 == 0:
            proj_a = _matmul(hn, hyb_w_in, li, A_WIDTH, F32)
            w_b = hyb_w_in[li, :, B_OFF:].astype(BF16)[None]
            proj_b = _matmul(hn, w_b, 0, B_WIDTH, F32)
            gcol, grow = _gates(hn, hyb_w_in[li, :, GATE_OFF:B_OFF], dn_a_log[li], dn_dt_bias[li], batch, seq)
            y_a = _deltanet(proj_a, dn_conv_t[li], gcol, grow, dn_norm_w[li], batch, seq)
            y_b = _dilated_attention(proj_b, cos_tab, sin_tab, batch, seq)
            h = _out_proj_residual([y_a, y_b], hyb_w_out, li, h)
        else:
            y = _shortconv_mix(hn, sc_w_in, li, sc_conv_t, seq)
            h = _out_proj_residual([y], sc_w_out, li, h)
    return _rmsnorm(h, final_norm_w, F32).reshape(batch, seq, d)
```

```python
import functools

import jax
import jax.numpy as jnp
from jax import lax
from jax.experimental import pallas as pl
from jax.experimental.pallas import tpu as pltpu

F32 = jnp.float32
BF16 = jnp.bfloat16

D_MODEL = 2048
DEPTH = 4
HEAD_DIM = 128
EPS = 1e-6
DN_QK_HEADS = 8
DN_V_HEADS = 16
DN_CONV = 4
DN_CHUNK = 128
SW_HEADS = 8
SW_DILATIONS = (1, 4, 16)
SW_BLOCK = 128
ROPE_THETA = 500000.0
ROPE_DIM = HEAD_DIM // 4
ROPE_HALF = ROPE_DIM // 2
SC_WIDTH = 3072
SC_CONV = 3

A_Q = DN_QK_HEADS * HEAD_DIM
A_V = DN_V_HEADS * HEAD_DIM
A_WIDTH = 2 * A_Q + 2 * A_V
GATE_OFF = A_WIDTH
B_OFF = A_WIDTH + 2 * DN_V_HEADS
B_WIDTH = (len(SW_DILATIONS) + 3) * SW_HEADS * HEAD_DIM
HYB_MIX = A_V + SW_HEADS * HEAD_DIM

VMEM_LIMIT = 56 * 1024 * 1024
NEG_BIG = -1e30

_NT = (((1,), (1,)), ((), ()))
_TN = (((0,), (0,)), ((), ()))


def _params(*sem):
    return pltpu.CompilerParams(dimension_semantics=sem, vmem_limit_bytes=VMEM_LIMIT)


def _rmsnorm_kernel(x_ref, w_ref, o_ref):
    x = x_ref[...]
    ms = jnp.mean(x * x, axis=-1, keepdims=True)
    o_ref[...] = (x * lax.rsqrt(ms + EPS) * w_ref[...]).astype(o_ref.dtype)


def _rmsnorm(x, w, out_dtype, tm=512):
    m, d = x.shape
    return pl.pallas_call(
        _rmsnorm_kernel,
        grid=(m // tm,),
        in_specs=[pl.BlockSpec((tm, d), lambda i: (i, 0)),
                  pl.BlockSpec((1, d), lambda i: (0, 0))],
        out_specs=pl.BlockSpec((tm, d), lambda i: (i, 0)),
        out_shape=jax.ShapeDtypeStruct((m, d), out_dtype),
        compiler_params=_params("arbitrary"),
        name="rmsnorm",
    )(x, w.reshape(1, d))


def _mm_kernel(x_ref, w_ref, o_ref, wb_ref):
    @pl.when(pl.program_id(1) == 0)
    def _():
        wb_ref[...] = w_ref[...].astype(BF16)

    o_ref[...] = jnp.dot(x_ref[...], wb_ref[...], preferred_element_type=F32).astype(o_ref.dtype)


def _matmul(x, w, layer, n_cols, out_dtype, tm=1024, tn=1024):
    m, k = x.shape
    return pl.pallas_call(
        _mm_kernel,
        grid=(n_cols // tn, m // tm),
        in_specs=[pl.BlockSpec((tm, k), lambda j, i: (i, 0)),
                  pl.BlockSpec((None, k, tn), lambda j, i: (layer, 0, j))],
        out_specs=pl.BlockSpec((tm, tn), lambda j, i: (i, j)),
        out_shape=jax.ShapeDtypeStruct((m, n_cols), out_dtype),
        scratch_shapes=[pltpu.VMEM((k, tn), BF16)],
        compiler_params=_params("arbitrary", "arbitrary"),
        name="in_proj",
    )(x, w)


def _mm_res_kernel(*refs, k_sizes):
    nx = len(k_sizes)
    x_refs = refs[:nx]
    w_ref, h_ref, o_ref, wb_ref = refs[nx:]

    @pl.when(pl.program_id(1) == 0)
    def _():
        wb_ref[...] = w_ref[...].astype(BF16)

    acc = h_ref[...]
    off = 0
    for x_ref, ks in zip(x_refs, k_sizes):
        acc = acc + jnp.dot(x_ref[...], wb_ref[off:off + ks, :], preferred_element_type=F32)
        off += ks
    o_ref[...] = acc


def _out_proj_residual(xs, w, layer, h, tm=512, tn=512):
    m, n = h.shape
    k_sizes = tuple(x.shape[1] for x in xs)
    k = sum(k_sizes)
    return pl.pallas_call(
        functools.partial(_mm_res_kernel, k_sizes=k_sizes),
        grid=(n // tn, m // tm),
        in_specs=[pl.BlockSpec((tm, ks), lambda j, i: (i, 0)) for ks in k_sizes]
        + [pl.BlockSpec((None, k, tn), lambda j, i: (layer, 0, j)),
           pl.BlockSpec((tm, tn), lambda j, i: (i, j))],
        out_specs=pl.BlockSpec((tm, tn), lambda j, i: (i, j)),
        out_shape=jax.ShapeDtypeStruct((m, n), F32),
        scratch_shapes=[pltpu.VMEM((k, tn), BF16)],
        compiler_params=_params("arbitrary", "arbitrary"),
        name="out_proj",
    )(*xs, w, h)


def _softplus(x):
    return jnp.maximum(x, 0.0) + jnp.log1p(jnp.exp(-jnp.abs(x)))


def _gates_kernel(x_ref, w_ref, wt_ref, alog_ref, dtb_ref, alog_c_ref, dtb_c_ref,
                  col_ref, row_ref, *, tm):
    nh = DN_V_HEADS
    c = DN_CHUNK
    x = x_ref[...]
    p = jnp.dot(x, w_ref[...], preferred_element_type=F32)
    beta = jax.nn.sigmoid(p[:, :nh])
    g = -jnp.exp(alog_ref[...]) * _softplus(p[:, nh:] + dtb_ref[...])
    ri = lax.broadcasted_iota(jnp.int32, (tm, tm), 0)
    ci = lax.broadcasted_iota(jnp.int32, (tm, tm), 1)
    same = (ri // c) == (ci // c)
    lower = jnp.where(same & (ci <= ri), 1.0, 0.0).astype(F32)
    gc = jnp.dot(lower, g, preferred_element_type=F32, precision=lax.Precision.HIGHEST)
    for hq in range(DN_QK_HEADS):
        col_ref[hq, :, 0:2] = beta[:, 2 * hq:2 * hq + 2]
        col_ref[hq, :, 2:4] = gc[:, 2 * hq:2 * hq + 2]
    pt = lax.dot_general(wt_ref[...], x, _NT, preferred_element_type=F32)
    gt = -jnp.exp(alog_c_ref[...]) * _softplus(pt[nh:, :] + dtb_c_ref[...])
    upper = jnp.where(same & (ri <= ci), 1.0, 0.0).astype(F32)
    gct = jnp.dot(gt, upper, preferred_element_type=F32, precision=lax.Precision.HIGHEST)
    for ch in range(tm // c):
        row_ref[ch] = gct[:, ch * c:(ch + 1) * c]


def _gates(hn, w_small, a_log, dt_bias, batch, seq, tm=512):
    m, k = hn.shape
    nh = DN_V_HEADS
    nch = seq // DN_CHUNK
    wb = w_small.astype(BF16)
    steps = seq // tm
    col, row = pl.pallas_call(
        functools.partial(_gates_kernel, tm=tm),
        grid=(m // tm,),
        in_specs=[pl.BlockSpec((tm, k), lambda i: (i, 0)),
                  pl.BlockSpec((k, 2 * nh), lambda i: (0, 0)),
                  pl.BlockSpec((2 * nh, k), lambda i: (0, 0)),
                  pl.BlockSpec((1, nh), lambda i: (0, 0)),
                  pl.BlockSpec((1, nh), lambda i: (0, 0)),
                  pl.BlockSpec((nh, 1), lambda i: (0, 0)),
                  pl.BlockSpec((nh, 1), lambda i: (0, 0))],
        out_specs=[pl.BlockSpec((None, DN_QK_HEADS, tm, 4), lambda i: (i // steps, 0, i % steps, 0)),
                   pl.BlockSpec((None, tm // DN_CHUNK, nh, DN_CHUNK), lambda i: (i // steps, i % steps, 0, 0))],
        out_shape=[jax.ShapeDtypeStruct((batch, DN_QK_HEADS, seq, 4), F32),
                   jax.ShapeDtypeStruct((batch, nch, nh, DN_CHUNK), F32)],
        compiler_params=_params("arbitrary"),
        name="dn_gates",
    )(hn, wb, wb.T, a_log.reshape(1, nh), dt_bias.reshape(1, nh),
      a_log.reshape(nh, 1), dt_bias.reshape(nh, 1))
    return col, row


def _conv_silu(ref, cw_ref, n, t0, width):
    c = DN_CHUNK
    cur = ref[pl.ds(t0, c), :]
    pstart = pl.multiple_of(jnp.maximum(t0 - 8, 0), 8)
    prev = jnp.where(n > 0, ref[pl.ds(pstart, 8), :], 0.0)
    xx = jnp.concatenate([prev, cur], axis=0)
    acc = cw_ref[width - 1:width, :] * cur
    for i in range(width - 1):
        lo = 8 - (width - 1) + i
        acc = acc + cw_ref[i:i + 1, :] * xx[lo:lo + c, :]
    return acc * jax.nn.sigmoid(acc)


def _l2norm(x):
    return x * lax.rsqrt(jnp.sum(x * x, axis=-1, keepdims=True) + EPS)


def _unit_lower_inverse_minus_eye(mats):
    c = mats[0].shape[0]
    ps = [-a for a in mats]
    rs = list(ps)
    span = 2
    while span < c:
        pbs = [p.astype(BF16) for p in ps]
        ps = [jnp.dot(pb, pb, preferred_element_type=F32) for pb in pbs]
        pbs = [p.astype(BF16) for p in ps]
        rps = [jnp.dot(r.astype(BF16), pb, preferred_element_type=F32) for r, pb in zip(rs, pbs)]
        rs = [r + p + rp for r, p, rp in zip(rs, ps, rps)]
        span *= 2
    return rs


def _deltanet_kernel(q_ref, k_ref, v_ref, z_ref, cwq_ref, cwk_ref, cwv_ref, gcol_ref, grow_ref, nw_ref,
                     o_ref, state_ref, gq_ref, cc_ref, op_ref, *, seq, unroll):
    c = DN_CHUNK
    dk = HEAD_DIM
    nch = seq // c
    hq = pl.program_id(1)
    ri = lax.broadcasted_iota(jnp.int32, (c, c), 0)
    ci = lax.broadcasted_iota(jnp.int32, (c, c), 1)
    causal = ri >= ci
    strict = ri > ci

    def prep_body(i, carry):
        chunks = [i * unroll + uu for uu in range(unroll)]
        t0s = [pl.multiple_of(n * c, c) for n in chunks]
        qs = [_l2norm(_conv_silu(q_ref, cwq_ref, n, t0, DN_CONV)) * (dk ** -0.5) for n, t0 in zip(chunks, t0s)]
        ks = [_l2norm(_conv_silu(k_ref, cwk_ref, n, t0, DN_CONV)) for n, t0 in zip(chunks, t0s)]
        v2s = [_conv_silu(v_ref, cwv_ref, n, t0, DN_CONV) for n, t0 in zip(chunks, t0s)]
        kbs = [k.astype(BF16) for k in ks]
        kks = [lax.dot_general(kb, kb, _NT, preferred_element_type=F32) for kb in kbs]
        qks = [lax.dot_general(q.astype(BF16), kb, _NT, preferred_element_type=F32) for q, kb in zip(qs, kbs)]
        gcols = [gcol_ref[pl.ds(t0, c), :] for t0 in t0s]
        chains = [(u, j) for u in range(unroll) for j in range(2)]
        betas = [gcols[u][:, j:j + 1] for u, j in chains]
        gcs = [gcols[u][:, 2 + j:3 + j] for u, j in chains]
        grs = [grow_ref[chunks[u], pl.ds(2 * hq + j, 1), :] for u, j in chains]
        decays = [jnp.exp(jnp.where(causal, gc - gr, NEG_BIG)) for gc, gr in zip(gcs, grs)]
        mats = [jnp.where(strict, beta * kks[u] * decay, 0.0) for (u, j), beta, decay in zip(chains, betas, decays)]
        rs = _unit_lower_inverse_minus_eye(mats)
        egs = [jnp.exp(gc) for gc in gcs]
        rhss = [jnp.concatenate([v2s[u][:, j * dk:(j + 1) * dk] * beta, ks[u] * (beta * eg)], axis=1)
                for (u, j), beta, eg in zip(chains, betas, egs)]
        uws = [(rhs + jnp.dot(r.astype(BF16), rhs.astype(BF16), preferred_element_type=F32)).astype(BF16)
               for r, rhs in zip(rs, rhss)]
        kds = [(ks[u] * jnp.exp(gc[c - 1:c, :] - gc)).astype(BF16) for (u, j), gc in zip(chains, gcs)]
        cgs = [lax.dot_general(kd, uw, _TN, preferred_element_type=F32) for kd, uw in zip(kds, uws)]
        oxs = [jnp.dot((qks[u] * decay).astype(BF16), uw, preferred_element_type=F32)
               for (u, j), decay, uw in zip(chains, decays, uws)]
        for (u, j), cg, ox, eg in zip(chains, cgs, oxs, egs):
            n = chunks[u]
            gq_ref[n, j, 0:c, :] = cg[:, dk:].astype(BF16)
            gq_ref[n, j, c:2 * c, :] = (qs[u] * eg - ox[:, dk:]).astype(BF16)
            cc_ref[n, j] = cg[:, :dk]
            op_ref[n, j] = ox[:, :dk]
        return carry

    lax.fori_loop(0, nch // unroll, prep_body, 0)

    state_ref[...] = jnp.zeros_like(state_ref)

    def scan_body(n, carry):
        t0 = pl.multiple_of(n * c, c)
        z2 = z_ref[pl.ds(t0, c), :]
        for j in range(2):
            s = state_ref[j]
            prod = jnp.dot(gq_ref[n, j], s.astype(BF16), preferred_element_type=F32)
            gr = grow_ref[n, pl.ds(2 * hq + j, 1), :]
            state_ref[j] = s * jnp.exp(gr[:, c - 1:c]) - prod[:c] + cc_ref[n, j]
            o = prod[c:] + op_ref[n, j]
            o = o * lax.rsqrt(jnp.mean(o * o, axis=-1, keepdims=True) + EPS) * nw_ref[...]
            z = z2[:, j * dk:(j + 1) * dk]
            o_ref[pl.ds(t0, c), j * dk:(j + 1) * dk] = (o * (z * jax.nn.sigmoid(z))).astype(o_ref.dtype)
        return carry

    lax.fori_loop(0, nch, scan_body, 0)


def _deltanet(proj_a, conv_w_t, gcol, grow, norm_w, batch, seq, unroll=4):
    dk = HEAD_DIM
    nq = DN_QK_HEADS
    c = DN_CHUNK
    nch = seq // c
    return pl.pallas_call(
        functools.partial(_deltanet_kernel, seq=seq, unroll=unroll),
        grid=(batch, nq),
        in_specs=[pl.BlockSpec((seq, dk), lambda b, h: (b, h)),
                  pl.BlockSpec((seq, dk), lambda b, h: (b, nq + h)),
                  pl.BlockSpec((seq, 2 * dk), lambda b, h: (b, nq + h)),
                  pl.BlockSpec((seq, 2 * dk), lambda b, h: (b, 2 * nq + h)),
                  pl.BlockSpec((DN_CONV, dk), lambda b, h: (0, h)),
                  pl.BlockSpec((DN_CONV, dk), lambda b, h: (0, nq + h)),
                  pl.BlockSpec((DN_CONV, 2 * dk), lambda b, h: (0, nq + h)),
                  pl.BlockSpec((None, None, seq, 4), lambda b, h: (b, h, 0, 0)),
                  pl.BlockSpec((None, nch, DN_V_HEADS, c), lambda b, h: (b, 0, 0, 0)),
                  pl.BlockSpec((1, dk), lambda b, h: (0, 0))],
        out_specs=pl.BlockSpec((seq, 2 * dk), lambda b, h: (b, h)),
        out_shape=jax.ShapeDtypeStruct((batch * seq, A_V), BF16),
        scratch_shapes=[pltpu.VMEM((2, dk, dk), F32),
                        pltpu.VMEM((nch, 2, 2 * c, dk), BF16),
                        pltpu.VMEM((nch, 2, dk, dk), F32),
                        pltpu.VMEM((nch, 2, c, dk), F32)],
        compiler_params=_params("arbitrary", "arbitrary"),
        name="deltanet",
    )(proj_a, proj_a, proj_a, proj_a, conv_w_t, conv_w_t, conv_w_t, gcol, grow, norm_w.reshape(1, dk))


def _rope_table_kernel(cos_ref, sin_ref):
    shape = cos_ref.shape
    pos = lax.broadcasted_iota(jnp.int32, shape, 0).astype(F32)
    lane = lax.broadcasted_iota(jnp.int32, shape, 1)
    idx = (lane % ROPE_HALF).astype(F32)
    inv = jnp.exp(idx * (-2.0 / ROPE_DIM) * jnp.log(jnp.float32(ROPE_THETA)))
    ang = pos * inv
    cos = jnp.cos(ang)
    sin = jnp.sin(ang)
    cos_ref[...] = jnp.where(lane < ROPE_DIM, cos, 1.0)
    sin_ref[...] = jnp.where(lane < ROPE_HALF, -sin, jnp.where(lane < ROPE_DIM, sin, 0.0))


def _rope_tables(seq):
    shp = jax.ShapeDtypeStruct((seq, HEAD_DIM), F32)
    return pl.pallas_call(_rope_table_kernel, out_shape=[shp, shp], name="rope_tables")()


def _attn_kernel(q0_ref, q1_ref, q2_ref, k_ref, v_ref, z_ref, cos_ref, sin_ref, o_ref,
                 qs_ref, ks_ref, m_ref, l_ref, acc_ref, *, seq):
    wb = SW_BLOCK
    dh = HEAD_DIM
    rows = 256
    lane = lax.broadcasted_iota(jnp.int32, (rows, dh), 1)

    def rope(x, cos, sin):
        swapped = jnp.where(lane < ROPE_HALF, pltpu.roll(x, dh - ROPE_HALF, 1), pltpu.roll(x, ROPE_HALF, 1))
        return x * cos + swapped * sin

    def rope_body(i, carry):
        r0 = pl.multiple_of(i * rows, rows)
        cos = cos_ref[pl.ds(r0, rows), :]
        sin = sin_ref[pl.ds(r0, rows), :]
        for g, q_ref in enumerate((q0_ref, q1_ref, q2_ref)):
            qs_ref[g, pl.ds(r0, rows), :] = rope(q_ref[pl.ds(r0, rows), :], cos, sin) * (dh ** -0.5)
        ks_ref[pl.ds(r0, rows), :] = rope(k_ref[pl.ds(r0, rows), :], cos, sin)
        return carry

    lax.fori_loop(0, seq // rows, rope_body, 0)

    qi = lax.broadcasted_iota(jnp.int32, (wb, wb), 0)
    kj = lax.broadcasted_iota(jnp.int32, (wb, wb), 1)
    cur_ok = kj <= qi
    prev_ok = kj >= qi

    for g, dil in enumerate(SW_DILATIONS):
        nb = max(seq // dil // wb, 1)
        has_prev = nb > 1

        def block_body(idx, carry, g=g, dil=dil, nb=nb, has_prev=has_prev):
            r = idx // nb
            n = idx % nb
            start = r + n * (dil * wb)

            def rows_of(ref, s):
                if dil == 1:
                    return ref[pl.ds(s, wb), :]
                return ref[pl.ds(s, wb, stride=dil), :]

            q = rows_of(qs_ref.at[g], start).astype(BF16)
            k_cur = rows_of(ks_ref, start).astype(BF16)
            v_cur = rows_of(v_ref, start).astype(BF16)
            s_cur = jnp.where(cur_ok, lax.dot_general(q, k_cur, _NT, preferred_element_type=F32), NEG_BIG)
            m_blk = jnp.max(s_cur, axis=-1, keepdims=True)
            if has_prev:
                pstart = jnp.maximum(start - dil * wb, 0)
                k_prev = rows_of(ks_ref, pstart).astype(BF16)
                v_prev = rows_of(v_ref, pstart).astype(BF16)
                ok = prev_ok & (n > 0)
                s_prev = jnp.where(ok, lax.dot_general(q, k_prev, _NT, preferred_element_type=F32), NEG_BIG)
                m_blk = jnp.maximum(m_blk, jnp.max(s_prev, axis=-1, keepdims=True))
            p_cur = jnp.exp(s_cur - m_blk)
            l_blk = jnp.sum(p_cur, axis=-1, keepdims=True)
            acc_blk = jnp.dot(p_cur.astype(BF16), v_cur, preferred_element_type=F32)
            if has_prev:
                p_prev = jnp.exp(s_prev - m_blk)
                l_blk = l_blk + jnp.sum(p_prev, axis=-1, keepdims=True)
                acc_blk = acc_blk + jnp.dot(p_prev.astype(BF16), v_prev, preferred_element_type=F32)
            m_blk = jnp.broadcast_to(m_blk, (wb, dh))
            l_blk = jnp.broadcast_to(l_blk, (wb, dh))
            if g == 0:
                m_ref[pl.ds(start, wb), :] = m_blk
                l_ref[pl.ds(start, wb), :] = l_blk
                acc_ref[pl.ds(start, wb), :] = acc_blk
            else:
                sl = pl.ds(start, wb, stride=dil)
                m_old = m_ref[sl, :]
                m_new = jnp.maximum(m_old, m_blk)
                a_old = jnp.exp(m_old - m_new)
                a_blk = jnp.exp(m_blk - m_new)
                m_ref[sl, :] = m_new
                l_ref[sl, :] = a_old * l_ref[sl, :] + a_blk * l_blk
                acc_ref[sl, :] = a_old * acc_ref[sl, :] + a_blk * acc_blk
            return carry

        lax.fori_loop(0, dil * nb, block_body, 0)

    def out_body(i, carry):
        r0 = pl.multiple_of(i * rows, rows)
        sl = pl.ds(r0, rows)
        z = z_ref[sl, :]
        o = acc_ref[sl, :] / l_ref[sl, :]
        o_ref[sl, :] = (o * (z * jax.nn.sigmoid(z))).astype(o_ref.dtype)
        return carry

    lax.fori_loop(0, seq // rows, out_body, 0)


def _dilated_attention(proj_b, cos_tab, sin_tab, batch, seq):
    dh = HEAD_DIM
    nh = SW_HEADS
    ng = len(SW_DILATIONS)

    def col(block):
        return pl.BlockSpec((seq, dh), lambda b, h: (b, block * nh + h))

    tab = pl.BlockSpec((seq, dh), lambda b, h: (0, 0))
    return pl.pallas_call(
        functools.partial(_attn_kernel, seq=seq),
        grid=(batch, nh),
        in_specs=[col(0), col(1), col(2), col(ng), col(ng + 1), col(ng + 2), tab, tab],
        out_specs=pl.BlockSpec((seq, dh), lambda b, h: (b, h)),
        out_shape=jax.ShapeDtypeStruct((batch * seq, nh * dh), BF16),
        scratch_shapes=[pltpu.VMEM((ng, seq, dh), F32), pltpu.VMEM((seq, dh), F32),
                        pltpu.VMEM((seq, dh), F32), pltpu.VMEM((seq, dh), F32), pltpu.VMEM((seq, dh), F32)],
        compiler_params=_params("arbitrary", "arbitrary"),
        name="dilated_attn",
    )(proj_b, proj_b, proj_b, proj_b, proj_b, proj_b, cos_tab, sin_tab)


def _shortconv_kernel(x_ref, wb_ref, wc_ref, wu_ref, wz_ref, cw_ref, o_ref, wq_ref, halo_ref, *, steps_per_seq):
    i = pl.program_id(1)

    @pl.when(i == 0)
    def _():
        for g, w_ref in enumerate((wb_ref, wc_ref, wu_ref, wz_ref)):
            wq_ref[g] = w_ref[...].astype(BF16)

    x = x_ref[...]
    tm = x.shape[0]
    gate_c = jnp.dot(x, wq_ref[1], preferred_element_type=F32)
    u = jnp.dot(x, wq_ref[2], preferred_element_type=F32)
    cu = gate_c * u
    prev = jnp.where(i % steps_per_seq != 0, halo_ref[...], 0.0)
    halo_ref[...] = cu[tm - 8:, :]
    xx = jnp.concatenate([prev, cu], axis=0)
    conv = cw_ref[SC_CONV - 1:SC_CONV, :] * cu
    for t in range(SC_CONV - 1):
        lo = 8 - (SC_CONV - 1) + t
        conv = conv + cw_ref[t:t + 1, :] * xx[lo:lo + tm, :]
    gate_b = jnp.dot(x, wq_ref[0], preferred_element_type=F32)
    z = jnp.dot(x, wq_ref[3], preferred_element_type=F32)
    o_ref[...] = (gate_b * conv * (z * jax.nn.sigmoid(z))).astype(o_ref.dtype)


def _shortconv_mix(hn, w_in, layer, conv_w_t, seq, tm=512, tn=256):
    m, k = hn.shape
    nblk = SC_WIDTH // tn

    def wspec(g):
        return pl.BlockSpec((None, k, tn), lambda j, i: (layer, 0, g * nblk + j))

    return pl.pallas_call(
        functools.partial(_shortconv_kernel, steps_per_seq=seq // tm),
        grid=(nblk, m // tm),
        in_specs=[pl.BlockSpec((tm, k), lambda j, i: (i, 0)), wspec(0), wspec(1), wspec(2), wspec(3),
                  pl.BlockSpec((None, SC_CONV, tn), lambda j, i: (layer, 0, j))],
        out_specs=pl.BlockSpec((tm, tn), lambda j, i: (i, j)),
        out_shape=jax.ShapeDtypeStruct((m, SC_WIDTH), BF16),
        scratch_shapes=[pltpu.VMEM((4, k, tn), BF16), pltpu.VMEM((8, tn), F32)],
        compiler_params=_params("arbitrary", "arbitrary"),
        name="shortconv_mix",
    )(hn, w_in, w_in, w_in, w_in, conv_w_t)


def kernel(x, norm_w, hyb_w_in, dn_conv_w, dn_a_log, dn_dt_bias, dn_norm_w, hyb_w_out, sc_w_in, sc_conv_w,
           sc_w_out, final_norm_w):
    batch, seq, d = x.shape
    h = x.reshape(batch * seq, d)
    cos_tab, sin_tab = _rope_tables(seq)
    dn_conv_t = jnp.swapaxes(dn_conv_w, 1, 2)
    sc_conv_t = jnp.swapaxes(sc_conv_w, 1, 2)
    for layer in range(DEPTH):
        hn = _rmsnorm(h, norm_w[layer], BF16)
        li = layer // 2
        if layer % 2 == 0:
            proj_a = _matmul(hn, hyb_w_in, li, A_WIDTH, F32)
            w_b = hyb_w_in[li, :, B_OFF:].astype(BF16)[None]
            proj_b = _matmul(hn, w_b, 0, B_WIDTH, F32)
            gcol, grow = _gates(hn, hyb_w_in[li, :, GATE_OFF:B_OFF], dn_a_log[li], dn_dt_bias[li], batch, seq)
            y_a = _deltanet(proj_a, dn_conv_t[li], gcol, grow, dn_norm_w[li], batch, seq)
            y_b = _dilated_attention(proj_b, cos_tab, sin_tab, batch, seq)
            h = _out_proj_residual([y_a, y_b], hyb_w_out, li, h)
        else:
            y = _shortconv_mix(hn, sc_w_in, li, sc_conv_t, seq)
            h = _out_proj_residual([y], sc_w_out, li, h)
    return _rmsnorm(h, final_norm_w, F32).reshape(batch, seq, d)
```

```python
import functools

import jax
import jax.numpy as jnp
from jax import lax
from jax.experimental import pallas as pl
from jax.experimental.pallas import tpu as pltpu

F32 = jnp.float32
BF16 = jnp.bfloat16

D_MODEL = 2048
DEPTH = 4
HEAD_DIM = 128
EPS = 1e-6
DN_QK_HEADS = 8
DN_V_HEADS = 16
DN_CONV = 4
DN_CHUNK = 128
SW_HEADS = 8
SW_DILATIONS = (1, 4, 16)
SW_BLOCK = 128
ROPE_THETA = 500000.0
ROPE_DIM = HEAD_DIM // 4
ROPE_HALF = ROPE_DIM // 2
SC_WIDTH = 3072
SC_CONV = 3

A_Q = DN_QK_HEADS * HEAD_DIM
A_V = DN_V_HEADS * HEAD_DIM
A_WIDTH = 2 * A_Q + 2 * A_V
GATE_OFF = A_WIDTH
B_OFF = A_WIDTH + 2 * DN_V_HEADS
B_WIDTH = (len(SW_DILATIONS) + 3) * SW_HEADS * HEAD_DIM
HYB_MIX = A_V + SW_HEADS * HEAD_DIM

LANES = 128
VMEM_LIMIT = 56 * 1024 * 1024
NEG_BIG = -1e30

_NT = (((1,), (1,)), ((), ()))
_TN = (((0,), (0,)), ((), ()))


def _params(*sem):
    return pltpu.CompilerParams(dimension_semantics=sem, vmem_limit_bytes=VMEM_LIMIT)


def _rmsnorm_kernel(x_ref, w_ref, o_ref):
    x = x_ref[...]
    ms = jnp.mean(x * x, axis=-1, keepdims=True)
    o_ref[...] = (x * lax.rsqrt(ms + EPS) * w_ref[...]).astype(o_ref.dtype)


def _rmsnorm(x, w, out_dtype, tm=512):
    m, d = x.shape
    return pl.pallas_call(
        _rmsnorm_kernel,
        grid=(m // tm,),
        in_specs=[pl.BlockSpec((tm, d), lambda i: (i, 0)),
                  pl.BlockSpec((1, d), lambda i: (0, 0))],
        out_specs=pl.BlockSpec((tm, d), lambda i: (i, 0)),
        out_shape=jax.ShapeDtypeStruct((m, d), out_dtype),
        compiler_params=_params("arbitrary"),
        name="rmsnorm",
    )(x, w.reshape(1, d))


def _mm_kernel(x_ref, w_ref, wx_ref, o_ref, wb_ref, *, shift):
    @pl.when(pl.program_id(1) == 0)
    def _():
        k, tn = wb_ref.shape
        rows = 256

        def cast_body(r, carry):
            r0 = pl.multiple_of(r * rows, rows)
            w = w_ref[pl.ds(r0, rows), :]
            if shift:
                w = jnp.concatenate([w, wx_ref[pl.ds(r0, rows), :]], axis=1)[:, shift:shift + tn]
            wb_ref[pl.ds(r0, rows), :] = w.astype(BF16)
            return carry

        lax.fori_loop(0, k // rows, cast_body, 0)

    o_ref[...] = jnp.dot(x_ref[...], wb_ref[...], preferred_element_type=F32).astype(o_ref.dtype)


def _matmul(x, w, layer, col0, n_cols, out_dtype, tm=1024, tn=1024):
    m, k = x.shape
    shift = col0 % LANES
    base = col0 // LANES
    per_tile = tn // LANES
    assert base % per_tile == 0
    return pl.pallas_call(
        functools.partial(_mm_kernel, shift=shift),
        grid=(n_cols // tn, m // tm),
        in_specs=[pl.BlockSpec((tm, k), lambda j, i: (i, 0)),
                  pl.BlockSpec((None, k, tn), lambda j, i: (layer, 0, base // per_tile + j)),
                  pl.BlockSpec((None, k, LANES), lambda j, i: (layer, 0, base + (j + 1) * per_tile))],
        out_specs=pl.BlockSpec((tm, tn), lambda j, i: (i, j)),
        out_shape=jax.ShapeDtypeStruct((m, n_cols), out_dtype),
        scratch_shapes=[pltpu.VMEM((k, tn), BF16)],
        compiler_params=_params("arbitrary", "arbitrary"),
        name="in_proj",
    )(x, w, w)


def _mm_res_kernel(*refs, k_sizes):
    nx = len(k_sizes)
    x_refs = refs[:nx]
    w_ref, h_ref, o_ref, wb_ref = refs[nx:]

    @pl.when(pl.program_id(1) == 0)
    def _():
        wb_ref[...] = w_ref[...].astype(BF16)

    acc = h_ref[...]
    off = 0
    for x_ref, ks in zip(x_refs, k_sizes):
        acc = acc + jnp.dot(x_ref[...], wb_ref[off:off + ks, :], preferred_element_type=F32)
        off += ks
    o_ref[...] = acc


def _out_proj_residual(xs, w, layer, h, tm=1024, tn=512):
    m, n = h.shape
    k_sizes = tuple(x.shape[1] for x in xs)
    k = sum(k_sizes)
    return pl.pallas_call(
        functools.partial(_mm_res_kernel, k_sizes=k_sizes),
        grid=(n // tn, m // tm),
        in_specs=[pl.BlockSpec((tm, ks), lambda j, i: (i, 0)) for ks in k_sizes]
        + [pl.BlockSpec((None, k, tn), lambda j, i: (layer, 0, j)),
           pl.BlockSpec((tm, tn), lambda j, i: (i, j))],
        out_specs=pl.BlockSpec((tm, tn), lambda j, i: (i, j)),
        out_shape=jax.ShapeDtypeStruct((m, n), F32),
        scratch_shapes=[pltpu.VMEM((k, tn), BF16)],
        compiler_params=_params("arbitrary", "arbitrary"),
        name="out_proj",
    )(*xs, w, h)


def _softplus(x):
    return jnp.maximum(x, 0.0) + jnp.log1p(jnp.exp(-jnp.abs(x)))


def _gates_kernel(x_ref, w_ref, alog_ref, dtb_ref, alog_c_ref, dtb_c_ref, col_ref, row_ref, ws_ref, wt_ref, *, tm):
    nh = DN_V_HEADS
    c = DN_CHUNK

    @pl.when(pl.program_id(0) == 0)
    def _():
        w = w_ref[...]
        ws_ref[...] = w.astype(BF16)
        wt_ref[...] = w.T.astype(BF16)

    x = x_ref[...]
    p = jnp.dot(x, ws_ref[...], preferred_element_type=F32)
    beta = jax.nn.sigmoid(p[:, :nh])
    g = -jnp.exp(alog_ref[...]) * _softplus(p[:, nh:2 * nh] + dtb_ref[...])
    ri = lax.broadcasted_iota(jnp.int32, (tm, tm), 0)
    ci = lax.broadcasted_iota(jnp.int32, (tm, tm), 1)
    same = (ri // c) == (ci // c)
    lower = jnp.where(same & (ci <= ri), 1.0, 0.0).astype(F32)
    gc = jnp.dot(lower, g, preferred_element_type=F32, precision=lax.Precision.HIGHEST)
    for hq in range(DN_QK_HEADS):
        col_ref[hq, :, 0:2] = beta[:, 2 * hq:2 * hq + 2]
        col_ref[hq, :, 2:4] = gc[:, 2 * hq:2 * hq + 2]
    pt = lax.dot_general(wt_ref[...], x, _NT, preferred_element_type=F32)
    gt = -jnp.exp(alog_c_ref[...]) * _softplus(pt[nh:2 * nh, :] + dtb_c_ref[...])
    upper = jnp.where(same & (ri <= ci), 1.0, 0.0).astype(F32)
    gct = jnp.dot(gt, upper, preferred_element_type=F32, precision=lax.Precision.HIGHEST)
    for ch in range(tm // c):
        row_ref[ch] = gct[:, ch * c:(ch + 1) * c]


def _gates(hn, w_in, layer, a_log, dt_bias, batch, seq, tm=512):
    m, k = hn.shape
    nh = DN_V_HEADS
    nch = seq // DN_CHUNK
    assert GATE_OFF % LANES == 0
    steps = seq // tm
    col, row = pl.pallas_call(
        functools.partial(_gates_kernel, tm=tm),
        grid=(m // tm,),
        in_specs=[pl.BlockSpec((tm, k), lambda i: (i, 0)),
                  pl.BlockSpec((None, k, LANES), lambda i: (layer, 0, GATE_OFF // LANES)),
                  pl.BlockSpec((1, nh), lambda i: (0, 0)),
                  pl.BlockSpec((1, nh), lambda i: (0, 0)),
                  pl.BlockSpec((nh, 1), lambda i: (0, 0)),
                  pl.BlockSpec((nh, 1), lambda i: (0, 0))],
        out_specs=[pl.BlockSpec((None, DN_QK_HEADS, tm, 4), lambda i: (i // steps, 0, i % steps, 0)),
                   pl.BlockSpec((None, tm // DN_CHUNK, nh, DN_CHUNK), lambda i: (i // steps, i % steps, 0, 0))],
        out_shape=[jax.ShapeDtypeStruct((batch, DN_QK_HEADS, seq, 4), F32),
                   jax.ShapeDtypeStruct((batch, nch, nh, DN_CHUNK), F32)],
        scratch_shapes=[pltpu.VMEM((k, LANES), BF16), pltpu.VMEM((LANES, k), BF16)],
        compiler_params=_params("arbitrary"),
        name="dn_gates",
    )(hn, w_in, a_log.reshape(1, nh), dt_bias.reshape(1, nh), a_log.reshape(nh, 1), dt_bias.reshape(nh, 1))
    return col, row


def _conv_silu(pad_ref, col0, ncols, cw_ref, t0, width):
    c = DN_CHUNK
    acc = None
    for i in range(width):
        parts = [pad_ref[col0 // HEAD_DIM + p, pl.ds(t0 + (8 - (width - 1) + i), c, stride=1), :]
                 for p in range(ncols // HEAD_DIM)]
        x = parts[0] if len(parts) == 1 else jnp.concatenate(parts, axis=1)
        term = cw_ref[i:i + 1, :] * x
        acc = term if acc is None else acc + term
    return acc * jax.nn.sigmoid(acc)


def _l2norm(x):
    return x * lax.rsqrt(jnp.sum(x * x, axis=-1, keepdims=True) + EPS)


def _unit_lower_inverse_minus_eye(mats):
    c = mats[0].shape[0]
    ri = lax.broadcasted_iota(jnp.int32, (c, c), 0)
    ci = lax.broadcasted_iota(jnp.int32, (c, c), 1)
    x = ri ^ ci
    rs = [jnp.where(x < 2, -a, 0.0) for a in mats]
    b = 2
    while b < c:
        level = (x >= b) & (x < 2 * b)
        ls = [jnp.where(level, a, 0.0) for a in mats]
        lbs = [l.astype(BF16) for l in ls]
        rbs = [r.astype(BF16) for r in rs]
        xs = [l + jnp.dot(rb, lb, preferred_element_type=F32) for l, rb, lb in zip(ls, rbs, lbs)]
        rs = [r - xm - jnp.dot(xm.astype(BF16), rb, preferred_element_type=F32) for r, xm, rb in zip(rs, xs, rbs)]
        b *= 2
    return rs


def _deltanet_kernel(q_ref, k_ref, v_ref, z_ref, cwq_ref, cwk_ref, cwv_ref, gcol_ref, grow_ref, nw_ref,
                     o_ref, state_ref, gq_ref, cc_ref, op_ref, pad_ref, *, seq, unroll):
    c = DN_CHUNK
    dk = HEAD_DIM
    nch = seq // c
    hq = pl.program_id(1)
    ri = lax.broadcasted_iota(jnp.int32, (c, c), 0)
    ci = lax.broadcasted_iota(jnp.int32, (c, c), 1)
    causal = ri >= ci
    strict = ri > ci

    pad_ref[:, 0:8, :] = jnp.zeros((4, 8, dk), F32)

    def pad_body(i, carry):
        r0 = pl.multiple_of(i * c, c)
        pad_ref[0, pl.ds(8 + r0, c), :] = q_ref[pl.ds(r0, c), :]
        pad_ref[1, pl.ds(8 + r0, c), :] = k_ref[pl.ds(r0, c), :]
        pad_ref[2, pl.ds(8 + r0, c), :] = v_ref[pl.ds(r0, c), 0:dk]
        pad_ref[3, pl.ds(8 + r0, c), :] = v_ref[pl.ds(r0, c), dk:2 * dk]
        return carry

    lax.fori_loop(0, nch, pad_body, 0)

    def prep_body(i, carry):
        chunks = [i * unroll + uu for uu in range(unroll)]
        t0s = [pl.multiple_of(n * c, c) for n in chunks]
        qs = [_l2norm(_conv_silu(pad_ref, 0, dk, cwq_ref, t0, DN_CONV)) * (dk ** -0.5) for t0 in t0s]
        ks = [_l2norm(_conv_silu(pad_ref, dk, dk, cwk_ref, t0, DN_CONV)) for t0 in t0s]
        v2s = [_conv_silu(pad_ref, 2 * dk, 2 * dk, cwv_ref, t0, DN_CONV) for t0 in t0s]
        kbs = [k.astype(BF16) for k in ks]
        kks = [lax.dot_general(kb, kb, _NT, preferred_element_type=F32) for kb in kbs]
        qks = [lax.dot_general(q.astype(BF16), kb, _NT, preferred_element_type=F32) for q, kb in zip(qs, kbs)]
        gcols = [gcol_ref[pl.ds(t0, c), :] for t0 in t0s]
        chains = [(u, j) for u in range(unroll) for j in range(2)]
        betas = [gcols[u][:, j:j + 1] for u, j in chains]
        gcs = [gcols[u][:, 2 + j:3 + j] for u, j in chains]
        grs = [grow_ref[chunks[u], pl.ds(2 * hq + j, 1), :] for u, j in chains]
        decays = [jnp.exp(jnp.where(causal, gc - gr, NEG_BIG)) for gc, gr in zip(gcs, grs)]
        mats = [jnp.where(strict, beta * kks[u] * decay, 0.0) for (u, j), beta, decay in zip(chains, betas, decays)]
        rs = _unit_lower_inverse_minus_eye(mats)
        egs = [jnp.exp(gc) for gc in gcs]
        rhss = [jnp.concatenate([v2s[u][:, j * dk:(j + 1) * dk] * beta, ks[u] * (beta * eg)], axis=1)
                for (u, j), beta, eg in zip(chains, betas, egs)]
        uws = [(rhs + jnp.dot(r.astype(BF16), rhs.astype(BF16), preferred_element_type=F32)).astype(BF16)
               for r, rhs in zip(rs, rhss)]
        kds = [(ks[u] * jnp.exp(gc[c - 1:c, :] - gc)).astype(BF16) for (u, j), gc in zip(chains, gcs)]
        cgs = [lax.dot_general(kd, uw, _TN, preferred_element_type=F32) for kd, uw in zip(kds, uws)]
        oxs = [jnp.dot((qks[u] * decay).astype(BF16), uw, preferred_element_type=F32)
               for (u, j), decay, uw in zip(chains, decays, uws)]
        for (u, j), cg, ox, eg in zip(chains, cgs, oxs, egs):
            n = chunks[u]
            gq_ref[n, j, 0:c, :] = cg[:, dk:].astype(BF16)
            gq_ref[n, j, c:2 * c, :] = (qs[u] * eg - ox[:, dk:]).astype(BF16)
            cc_ref[n, j] = cg[:, :dk]
            op_ref[n, j] = ox[:, :dk]
        return carry

    lax.fori_loop(0, nch // unroll, prep_body, 0)

    state_ref[...] = jnp.zeros_like(state_ref)

    def scan_body(n, carry):
        t0 = pl.multiple_of(n * c, c)
        z2 = z_ref[pl.ds(t0, c), :]
        for j in range(2):
            s = state_ref[j]
            prod = jnp.dot(gq_ref[n, j], s.astype(BF16), preferred_element_type=F32)
            gr = grow_ref[n, pl.ds(2 * hq + j, 1), :]
            state_ref[j] = s * jnp.exp(gr[:, c - 1:c]) - prod[:c] + cc_ref[n, j]
            o = prod[c:] + op_ref[n, j]
            o = o * lax.rsqrt(jnp.mean(o * o, axis=-1, keepdims=True) + EPS) * nw_ref[...]
            z = z2[:, j * dk:(j + 1) * dk]
            o_ref[pl.ds(t0, c), j * dk:(j + 1) * dk] = (o * (z * jax.nn.sigmoid(z))).astype(o_ref.dtype)
        return carry

    lax.fori_loop(0, nch, scan_body, 0)


def _deltanet(proj_a, conv_w_t, gcol, grow, norm_w, batch, seq, unroll=4):
    dk = HEAD_DIM
    nq = DN_QK_HEADS
    c = DN_CHUNK
    nch = seq // c
    return pl.pallas_call(
        functools.partial(_deltanet_kernel, seq=seq, unroll=unroll),
        grid=(batch, nq),
        in_specs=[pl.BlockSpec((seq, dk), lambda b, h: (b, h)),
                  pl.BlockSpec((seq, dk), lambda b, h: (b, nq + h)),
                  pl.BlockSpec((seq, 2 * dk), lambda b, h: (b, nq + h)),
                  pl.BlockSpec((seq, 2 * dk), lambda b, h: (b, 2 * nq + h)),
                  pl.BlockSpec((DN_CONV, dk), lambda b, h: (0, h)),
                  pl.BlockSpec((DN_CONV, dk), lambda b, h: (0, nq + h)),
                  pl.BlockSpec((DN_CONV, 2 * dk), lambda b, h: (0, nq + h)),
                  pl.BlockSpec((None, None, seq, 4), lambda b, h: (b, h, 0, 0)),
                  pl.BlockSpec((None, nch, DN_V_HEADS, c), lambda b, h: (b, 0, 0, 0)),
                  pl.BlockSpec((1, dk), lambda b, h: (0, 0))],
        out_specs=pl.BlockSpec((seq, 2 * dk), lambda b, h: (b, h)),
        out_shape=jax.ShapeDtypeStruct((batch * seq, A_V), BF16),
        scratch_shapes=[pltpu.VMEM((2, dk, dk), F32),
                        pltpu.VMEM((nch, 2, 2 * c, dk), BF16),
                        pltpu.VMEM((nch, 2, dk, dk), F32),
                        pltpu.VMEM((nch, 2, c, dk), F32),
                        pltpu.VMEM((4, 8 + seq, dk), F32)],
        compiler_params=_params("arbitrary", "arbitrary"),
        name="deltanet",
    )(proj_a, proj_a, proj_a, proj_a, conv_w_t, conv_w_t, conv_w_t, gcol, grow, norm_w.reshape(1, dk))


def _rope_table_kernel(cos_ref, sin_ref):
    shape = cos_ref.shape
    pos = lax.broadcasted_iota(jnp.int32, shape, 0).astype(F32)
    lane = lax.broadcasted_iota(jnp.int32, shape, 1)
    idx = (lane % ROPE_HALF).astype(F32)
    inv = jnp.exp(idx * (-2.0 / ROPE_DIM) * jnp.log(jnp.float32(ROPE_THETA)))
    ang = pos * inv
    cos = jnp.cos(ang)
    sin = jnp.sin(ang)
    cos_ref[...] = jnp.where(lane < ROPE_DIM, cos, 1.0)
    sin_ref[...] = jnp.where(lane < ROPE_HALF, -sin, jnp.where(lane < ROPE_DIM, sin, 0.0))


def _rope_tables(seq):
    shp = jax.ShapeDtypeStruct((seq, HEAD_DIM), F32)
    return pl.pallas_call(_rope_table_kernel, out_shape=[shp, shp], name="rope_tables")()


def _attn_kernel(q0_ref, q1_ref, q2_ref, k_ref, v_ref, z_ref, cos_ref, sin_ref, o_ref,
                 qs_ref, ks_ref, m_ref, l_ref, acc_ref, bias_ref, *, seq, unroll):
    wb = SW_BLOCK
    dh = HEAD_DIM
    rows = 256
    pr = lax.broadcasted_iota(jnp.int32, (dh, dh), 0)
    pc = lax.broadcasted_iota(jnp.int32, (dh, dh), 1)
    swap = jnp.where(((pc < ROPE_HALF) & (pr == pc + ROPE_HALF))
                     | ((pc >= ROPE_HALF) & (pc < ROPE_DIM) & (pr == pc - ROPE_HALF)), 1.0, 0.0).astype(BF16)

    def rope(x, cos, sin):
        swapped = jnp.dot(x.astype(BF16), swap, preferred_element_type=F32)
        return x * cos + swapped * sin

    def rope_body(i, carry):
        r0 = pl.multiple_of(i * rows, rows)
        cos = cos_ref[pl.ds(r0, rows), :]
        sin = sin_ref[pl.ds(r0, rows), :]
        for g, q_ref in enumerate((q0_ref, q1_ref, q2_ref)):
            qs_ref[g, pl.ds(r0, rows), :] = rope(q_ref[pl.ds(r0, rows), :], cos, sin) * (dh ** -0.5)
        ks_ref[pl.ds(r0, rows), :] = rope(k_ref[pl.ds(r0, rows), :], cos, sin)
        return carry

    lax.fori_loop(0, seq // rows, rope_body, 0)

    delta = lax.broadcasted_iota(jnp.int32, (wb, 2 * wb), 1) - lax.broadcasted_iota(jnp.int32, (wb, 2 * wb), 0)
    bias_ref[0] = jnp.where((delta >= 0) & (delta <= wb), 0.0, NEG_BIG)
    bias_ref[1] = jnp.where(delta <= 0, 0.0, NEG_BIG)

    order = sorted(range(len(SW_DILATIONS)), key=lambda gi: -SW_DILATIONS[gi])
    for pos_in_order, g in enumerate(order):
        dil = SW_DILATIONS[g]
        first = pos_in_order == 0
        last = pos_in_order == len(order) - 1
        nb = max(seq // dil // wb, 1)
        has_prev = nb > 1
        kw = 2 * wb if has_prev else wb
        ones = jnp.ones((kw, dh), BF16)

        def block_body(it, carry, g=g, dil=dil, nb=nb, has_prev=has_prev, kw=kw, ones=ones,
                       first=first, last=last):
            def rows_of(ref, s, count):
                if dil == 1:
                    return ref[pl.ds(s, count), :]
                return ref[pl.ds(s, count, stride=dil), :]

            idxs = [it * unroll + u for u in range(unroll)]
            ns = [idx & (nb - 1) for idx in idxs]
            starts = [lax.shift_right_logical(idx, nb.bit_length() - 1) + n * (dil * wb) for idx, n in zip(idxs, ns)]
            if dil == 1:
                starts = [pl.multiple_of(s, wb) for s in starts]
            if has_prev:
                kstarts = [jnp.where(n > 0, s - dil * wb, s) for s, n in zip(starts, ns)]
                biases = [bias_ref[jnp.where(n > 0, 0, 1)] for n in ns]
            else:
                kstarts = starts
                biases = [bias_ref[1, :, 0:wb] for _ in ns]
            qs = [rows_of(qs_ref.at[g], s, wb).astype(BF16) for s in starts]
            kwins = [rows_of(ks_ref, s, kw).astype(BF16) for s in kstarts]
            vaugs = [jnp.concatenate([rows_of(v_ref, s, kw).astype(BF16), ones], axis=1) for s in kstarts]
            scs = [lax.dot_general(q, kwin, _NT, preferred_element_type=F32) + bias
                   for bias, q, kwin in zip(biases, qs, kwins)]
            m_blks = [jnp.max(sc, axis=-1, keepdims=True) for sc in scs]
            ps = [jnp.exp(sc - m).astype(BF16) for sc, m in zip(scs, m_blks)]
            pvs = [jnp.dot(p, vaug, preferred_element_type=F32) for p, vaug in zip(ps, vaugs)]
            for s, m_blk, pv in zip(starts, m_blks, pvs):
                m_blk = jnp.broadcast_to(m_blk, (wb, dh))
                acc_blk = pv[:, :dh]
                l_blk = pv[:, dh:]
                sl = pl.ds(s, wb) if dil == 1 else pl.ds(s, wb, stride=dil)
                if first:
                    m_ref[sl, :] = m_blk
                    l_ref[sl, :] = l_blk
                    acc_ref[sl, :] = acc_blk
                    continue
                m_old = m_ref[sl, :]
                m_new = jnp.maximum(m_old, m_blk)
                a_old = jnp.exp(m_old - m_new)
                a_blk = jnp.exp(m_blk - m_new)
                l_new = a_old * l_ref[sl, :] + a_blk * l_blk
                acc_new = a_old * acc_ref[sl, :] + a_blk * acc_blk
                if last:
                    z = z_ref[sl, :]
                    o_ref[sl, :] = (acc_new / l_new * (z * jax.nn.sigmoid(z))).astype(o_ref.dtype)
                else:
                    m_ref[sl, :] = m_new
                    l_ref[sl, :] = l_new
                    acc_ref[sl, :] = acc_new
            return carry

        lax.fori_loop(0, dil * nb // unroll, block_body, 0)


def _dilated_attention(proj_b, cos_tab, sin_tab, batch, seq, unroll=8):
    dh = HEAD_DIM
    nh = SW_HEADS
    ng = len(SW_DILATIONS)

    def col(block):
        return pl.BlockSpec((seq, dh), lambda b, h: (b, block * nh + h))

    tab = pl.BlockSpec((seq, dh), lambda b, h: (0, 0))
    return pl.pallas_call(
        functools.partial(_attn_kernel, seq=seq, unroll=unroll),
        grid=(batch, nh),
        in_specs=[col(0), col(1), col(2), col(ng), col(ng + 1), col(ng + 2), tab, tab],
        out_specs=pl.BlockSpec((seq, dh), lambda b, h: (b, h)),
        out_shape=jax.ShapeDtypeStruct((batch * seq, nh * dh), BF16),
        scratch_shapes=[pltpu.VMEM((ng, seq, dh), F32), pltpu.VMEM((seq, dh), F32),
                        pltpu.VMEM((seq, dh), F32), pltpu.VMEM((seq, dh), F32), pltpu.VMEM((seq, dh), F32),
                        pltpu.VMEM((2, SW_BLOCK, 2 * SW_BLOCK), F32)],
        compiler_params=_params("arbitrary", "arbitrary"),
        name="dilated_attn",
    )(proj_b, proj_b, proj_b, proj_b, proj_b, proj_b, cos_tab, sin_tab)


def _shortconv_kernel(x_ref, wb_ref, wc_ref, wu_ref, wz_ref, cw_ref, o_ref, wq_ref, halo_ref, *, steps_per_seq):
    i = pl.program_id(1)

    @pl.when(i == 0)
    def _():
        for g, w_ref in enumerate((wb_ref, wc_ref, wu_ref, wz_ref)):
            wq_ref[g] = w_ref[...].astype(BF16)

    x = x_ref[...]
    tm = x.shape[0]
    gate_c = jnp.dot(x, wq_ref[1], preferred_element_type=F32)
    u = jnp.dot(x, wq_ref[2], preferred_element_type=F32)
    cu = gate_c * u
    prev = jnp.where(i % steps_per_seq != 0, halo_ref[...], 0.0)
    halo_ref[...] = cu[tm - 8:, :]
    xx = jnp.concatenate([prev, cu], axis=0)
    conv = cw_ref[SC_CONV - 1:SC_CONV, :] * cu
    for t in range(SC_CONV - 1):
        lo = 8 - (SC_CONV - 1) + t
        conv = conv + cw_ref[t:t + 1, :] * xx[lo:lo + tm, :]
    gate_b = jnp.dot(x, wq_ref[0], preferred_element_type=F32)
    z = jnp.dot(x, wq_ref[3], preferred_element_type=F32)
    o_ref[...] = (gate_b * conv * (z * jax.nn.sigmoid(z))).astype(o_ref.dtype)


def _shortconv_mix(hn, w_in, layer, conv_w_t, seq, tm=1024, tn=256):
    m, k = hn.shape
    nblk = SC_WIDTH // tn

    def wspec(g):
        return pl.BlockSpec((None, k, tn), lambda j, i: (layer, 0, g * nblk + j))

    return pl.pallas_call(
        functools.partial(_shortconv_kernel, steps_per_seq=seq // tm),
        grid=(nblk, m // tm),
        in_specs=[pl.BlockSpec((tm, k), lambda j, i: (i, 0)), wspec(0), wspec(1), wspec(2), wspec(3),
                  pl.BlockSpec((None, SC_CONV, tn), lambda j, i: (layer, 0, j))],
        out_specs=pl.BlockSpec((tm, tn), lambda j, i: (i, j)),
        out_shape=jax.ShapeDtypeStruct((m, SC_WIDTH), BF16),
        scratch_shapes=[pltpu.VMEM((4, k, tn), BF16), pltpu.VMEM((8, tn), F32)],
        compiler_params=_params("arbitrary", "arbitrary"),
        name="shortconv_mix",
    )(hn, w_in, w_in, w_in, w_in, conv_w_t)


def kernel(x, norm_w, hyb_w_in, dn_conv_w, dn_a_log, dn_dt_bias, dn_norm_w, hyb_w_out, sc_w_in, sc_conv_w,
           sc_w_out, final_norm_w):
    batch, seq, d = x.shape
    h = x.reshape(batch * seq, d)
    cos_tab, sin_tab = _rope_tables(seq)
    dn_conv_t = jnp.swapaxes(dn_conv_w, 1, 2)
    sc_conv_t = jnp.swapaxes(sc_conv_w, 1, 2)
    for layer in range(DEPTH):
        hn = _rmsnorm(h, norm_w[layer], BF16)
        li = layer // 2
        if layer % 2 == 0:
            proj_a = _matmul(hn, hyb_w_in, li, 0, A_WIDTH, F32)
            proj_b = _matmul(hn, hyb_w_in, li, B_OFF, B_WIDTH, F32)
            gcol, grow = _gates(hn, hyb_w_in, li, dn_a_log[li], dn_dt_bias[li], batch, seq)
            y_a = _deltanet(proj_a, dn_conv_t[li], gcol, grow, dn_norm_w[li], batch, seq)
            y_b = _dilated_attention(proj_b, cos_tab, sin_tab, batch, seq)
            h = _out_proj_residual([y_a, y_b], hyb_w_out, li, h)
        else:
            y = _shortconv_mix(hn, sc_w_in, li, sc_conv_t, seq)
            h = _out_proj_residual([y], sc_w_out, li, h)
    return _rmsnorm(h, final_norm_w, F32).reshape(batch, seq, d)
```

```python
import functools

import jax
import jax.numpy as jnp
from jax import lax
from jax.experimental import pallas as pl
from jax.experimental.pallas import tpu as pltpu

F32 = jnp.float32
BF16 = jnp.bfloat16

D_MODEL = 2048
DEPTH = 4
HEAD_DIM = 128
EPS = 1e-6
DN_QK_HEADS = 8
DN_V_HEADS = 16
DN_CONV = 4
DN_CHUNK = 128
SW_HEADS = 8
SW_DILATIONS = (1, 4, 16)
SW_BLOCK = 128
ROPE_THETA = 500000.0
ROPE_DIM = HEAD_DIM // 4
ROPE_HALF = ROPE_DIM // 2
SC_WIDTH = 3072
SC_CONV = 3

A_Q = DN_QK_HEADS * HEAD_DIM
A_V = DN_V_HEADS * HEAD_DIM
A_WIDTH = 2 * A_Q + 2 * A_V
GATE_OFF = A_WIDTH
B_OFF = A_WIDTH + 2 * DN_V_HEADS
B_WIDTH = (len(SW_DILATIONS) + 3) * SW_HEADS * HEAD_DIM
HYB_MIX = A_V + SW_HEADS * HEAD_DIM

LANES = 128
VMEM_LIMIT = 56 * 1024 * 1024
NEG_BIG = -1e30

_NT = (((1,), (1,)), ((), ()))
_TN = (((0,), (0,)), ((), ()))


def _params(*sem):
    return pltpu.CompilerParams(dimension_semantics=sem, vmem_limit_bytes=VMEM_LIMIT)


def _rmsnorm_kernel(x_ref, w_ref, o_ref):
    x = x_ref[...]
    ms = jnp.mean(x * x, axis=-1, keepdims=True)
    o_ref[...] = (x * lax.rsqrt(ms + EPS) * w_ref[...]).astype(o_ref.dtype)


def _rmsnorm(x, w, out_dtype, tm=512):
    m, d = x.shape
    return pl.pallas_call(
        _rmsnorm_kernel,
        grid=(m // tm,),
        in_specs=[pl.BlockSpec((tm, d), lambda i: (i, 0)),
                  pl.BlockSpec((1, d), lambda i: (0, 0))],
        out_specs=pl.BlockSpec((tm, d), lambda i: (i, 0)),
        out_shape=jax.ShapeDtypeStruct((m, d), out_dtype),
        compiler_params=_params("arbitrary"),
        name="rmsnorm",
    )(x, w.reshape(1, d))


def _mm_nt_kernel(x_ref, w_ref, wx_ref, o_ref, wb_ref, *, shift):
    @pl.when(pl.program_id(1) == 0)
    def _():
        tn = wb_ref.shape[0]
        wb_ref[0:tn - shift, :] = w_ref[shift:tn, :].astype(BF16)
        if shift:
            wb_ref[tn - shift:tn, :] = wx_ref[...].astype(BF16)

    o_ref[...] = lax.dot_general(x_ref[...], wb_ref[...], _NT, preferred_element_type=F32).astype(o_ref.dtype)


def _matmul_nt(x, wt, layer, row0, n_rows, out_dtype, tm=1024, tn=1024, shift_rows=32):
    m, k = x.shape
    shift = row0 % tn
    assert shift in (0, shift_rows) and tn % shift_rows == 0
    base = row0 // tn
    return pl.pallas_call(
        functools.partial(_mm_nt_kernel, shift=shift),
        grid=(n_rows // tn, m // tm),
        in_specs=[pl.BlockSpec((tm, k), lambda j, i: (i, 0)),
                  pl.BlockSpec((None, tn, k), lambda j, i: (layer, base + j, 0)),
                  pl.BlockSpec((None, shift_rows, k), lambda j, i: (layer, (base + j + 1) * (tn // shift_rows), 0))],
        out_specs=pl.BlockSpec((tm, tn), lambda j, i: (i, j)),
        out_shape=jax.ShapeDtypeStruct((m, n_rows), out_dtype),
        scratch_shapes=[pltpu.VMEM((tn, k), BF16)],
        compiler_params=_params("arbitrary", "arbitrary"),
        name="in_proj",
    )(x, wt, wt)


def _cast_kernel(x_ref, o_ref):
    o_ref[...] = x_ref[...].astype(o_ref.dtype)


def _cast_bf16(w, layer, rows=512):
    _, k, n = w.shape
    return pl.pallas_call(
        _cast_kernel,
        grid=(k // rows,),
        in_specs=[pl.BlockSpec((None, rows, n), lambda i: (layer, i, 0))],
        out_specs=pl.BlockSpec((rows, n), lambda i: (i, 0)),
        out_shape=jax.ShapeDtypeStruct((k, n), BF16),
        compiler_params=_params("arbitrary"),
        name="cast_bf16",
    )(w)


def _out_proj_kernel(*refs, k_sizes, emit_h):
    nx = len(k_sizes)
    x_refs = refs[:nx]
    w_ref, h_ref, nw_ref = refs[nx:nx + 3]
    out_refs = refs[nx + 3:]
    acc = h_ref[...]
    off = 0
    for x_ref, ks in zip(x_refs, k_sizes):
        acc = acc + jnp.dot(x_ref[...], w_ref[off:off + ks, :], preferred_element_type=F32)
        off += ks
    if emit_h:
        out_refs[0][...] = acc
    hn_ref = out_refs[-1]
    ms = jnp.mean(acc * acc, axis=-1, keepdims=True)
    hn_ref[...] = (acc * lax.rsqrt(ms + EPS) * nw_ref[...]).astype(hn_ref.dtype)


def _out_proj_residual_norm(xs, w_bf16, h, norm_w, last, tm=512):
    m, n = h.shape
    k_sizes = tuple(x.shape[1] for x in xs)
    k = sum(k_sizes)
    row = pl.BlockSpec((tm, n), lambda i: (i, 0))
    if last:
        out_shape = [jax.ShapeDtypeStruct((m, n), F32)]
    else:
        out_shape = [jax.ShapeDtypeStruct((m, n), F32), jax.ShapeDtypeStruct((m, n), BF16)]
    outs = pl.pallas_call(
        functools.partial(_out_proj_kernel, k_sizes=k_sizes, emit_h=not last),
        grid=(m // tm,),
        in_specs=[pl.BlockSpec((tm, ks), lambda i: (i, 0)) for ks in k_sizes]
        + [pl.BlockSpec((k, n), lambda i: (0, 0), pipeline_mode=pl.Buffered(1)), row,
           pl.BlockSpec((1, n), lambda i: (0, 0))],
        out_specs=[row] * len(out_shape),
        out_shape=out_shape,
        compiler_params=_params("arbitrary"),
        name="out_proj",
    )(*xs, w_bf16, h, norm_w.reshape(1, n))
    return outs


def _softplus(x):
    return jnp.maximum(x, 0.0) + jnp.log1p(jnp.exp(-jnp.abs(x)))


def _gates_kernel(x_ref, w_ref, alog_ref, dtb_ref, alog_c_ref, dtb_c_ref, col_ref, row_ref, ws_ref, *, tm):
    nh = DN_V_HEADS
    c = DN_CHUNK

    @pl.when(pl.program_id(0) == 0)
    def _():
        ws_ref[...] = w_ref[...].astype(BF16)

    x = x_ref[...]
    p = lax.dot_general(x, ws_ref[...], _NT, preferred_element_type=F32)
    beta = jax.nn.sigmoid(p[:, :nh])
    g = -jnp.exp(alog_ref[...]) * _softplus(p[:, nh:2 * nh] + dtb_ref[...])
    ri = lax.broadcasted_iota(jnp.int32, (tm, tm), 0)
    ci = lax.broadcasted_iota(jnp.int32, (tm, tm), 1)
    same = (ri // c) == (ci // c)
    lower = jnp.where(same & (ci <= ri), 1.0, 0.0).astype(F32)
    gc = jnp.dot(lower, g, preferred_element_type=F32, precision=lax.Precision.HIGHEST)
    for hq in range(DN_QK_HEADS):
        col_ref[hq, :, 0:2] = beta[:, 2 * hq:2 * hq + 2]
        col_ref[hq, :, 2:4] = gc[:, 2 * hq:2 * hq + 2]
    pt = lax.dot_general(ws_ref[...], x, _NT, preferred_element_type=F32)
    gt = -jnp.exp(alog_c_ref[...]) * _softplus(pt[nh:2 * nh, :] + dtb_c_ref[...])
    upper = jnp.where(same & (ri <= ci), 1.0, 0.0).astype(F32)
    gct = jnp.dot(gt, upper, preferred_element_type=F32, precision=lax.Precision.HIGHEST)
    for ch in range(tm // c):
        row_ref[ch] = gct[:, ch * c:(ch + 1) * c]


def _gates(hn, wt, layer, a_log, dt_bias, batch, seq, tm=512):
    m, k = hn.shape
    nh = DN_V_HEADS
    nch = seq // DN_CHUNK
    ng = 2 * nh
    assert GATE_OFF % ng == 0
    steps = seq // tm
    col, row = pl.pallas_call(
        functools.partial(_gates_kernel, tm=tm),
        grid=(m // tm,),
        in_specs=[pl.BlockSpec((tm, k), lambda i: (i, 0)),
                  pl.BlockSpec((None, ng, k), lambda i: (layer, GATE_OFF // ng, 0)),
                  pl.BlockSpec((1, nh), lambda i: (0, 0)),
                  pl.BlockSpec((1, nh), lambda i: (0, 0)),
                  pl.BlockSpec((nh, 1), lambda i: (0, 0)),
                  pl.BlockSpec((nh, 1), lambda i: (0, 0))],
        out_specs=[pl.BlockSpec((None, DN_QK_HEADS, tm, 4), lambda i: (i // steps, 0, i % steps, 0)),
                   pl.BlockSpec((None, tm // DN_CHUNK, nh, DN_CHUNK), lambda i: (i // steps, i % steps, 0, 0))],
        out_shape=[jax.ShapeDtypeStruct((batch, DN_QK_HEADS, seq, 4), F32),
                   jax.ShapeDtypeStruct((batch, nch, nh, DN_CHUNK), F32)],
        scratch_shapes=[pltpu.VMEM((ng, k), BF16)],
        compiler_params=_params("arbitrary"),
        name="dn_gates",
    )(hn, wt, a_log.reshape(1, nh), dt_bias.reshape(1, nh), a_log.reshape(nh, 1), dt_bias.reshape(nh, 1))
    return col, row


def _conv_silu(pad_ref, col0, ncols, cw_ref, t0, width):
    c = DN_CHUNK
    acc = None
    for i in range(width):
        parts = [pad_ref[col0 // HEAD_DIM + p, pl.ds(t0 + (8 - (width - 1) + i), c, stride=1), :]
                 for p in range(ncols // HEAD_DIM)]
        x = parts[0] if len(parts) == 1 else jnp.concatenate(parts, axis=1)
        term = cw_ref[i:i + 1, :] * x
        acc = term if acc is None else acc + term
    return acc * jax.nn.sigmoid(acc)


def _l2norm(x):
    return x * lax.rsqrt(jnp.sum(x * x, axis=-1, keepdims=True) + EPS)


def _unit_lower_inverse_minus_eye(mats):
    c = mats[0].shape[0]
    ri = lax.broadcasted_iota(jnp.int32, (c, c), 0)
    ci = lax.broadcasted_iota(jnp.int32, (c, c), 1)
    x = ri ^ ci
    rs = [jnp.where(x < 2, -a, 0.0) for a in mats]
    b = 2
    while b < c:
        level = (x >= b) & (x < 2 * b)
        ls = [jnp.where(level, a, 0.0) for a in mats]
        lbs = [l.astype(BF16) for l in ls]
        rbs = [r.astype(BF16) for r in rs]
        xs = [l + jnp.dot(rb, lb, preferred_element_type=F32) for l, rb, lb in zip(ls, rbs, lbs)]
        yield
        rs = [r - xm - jnp.dot(xm.astype(BF16), rb, preferred_element_type=F32) for r, xm, rb in zip(rs, xs, rbs)]
        yield
        b *= 2
    return rs


def _deltanet_kernel(q_ref, k_ref, v_ref, z_ref, cwq_ref, cwk_ref, cwv_ref, gcol_ref, grow_ref, nw_ref,
                     o_ref, state_ref, gq_ref, cc_ref, op_ref, pad_ref, *, seq, unroll, scan_every):
    c = DN_CHUNK
    dk = HEAD_DIM
    nch = seq // c
    hq = pl.program_id(1)
    ri = lax.broadcasted_iota(jnp.int32, (c, c), 0)
    ci = lax.broadcasted_iota(jnp.int32, (c, c), 1)
    causal = ri >= ci
    strict = ri > ci

    pad_ref[:, 0:8, :] = jnp.zeros((4, 8, dk), F32)

    def pad_body(i, carry):
        r0 = pl.multiple_of(i * c, c)
        pad_ref[0, pl.ds(8 + r0, c), :] = q_ref[pl.ds(r0, c), :].astype(F32)
        pad_ref[1, pl.ds(8 + r0, c), :] = k_ref[pl.ds(r0, c), :].astype(F32)
        pad_ref[2, pl.ds(8 + r0, c), :] = v_ref[pl.ds(r0, c), 0:dk].astype(F32)
        pad_ref[3, pl.ds(8 + r0, c), :] = v_ref[pl.ds(r0, c), dk:2 * dk].astype(F32)
        return carry

    lax.fori_loop(0, nch, pad_body, 0)

    def prep_stages(i):
        chunks = [i * unroll + uu for uu in range(unroll)]
        t0s = [pl.multiple_of(n * c, c) for n in chunks]
        qs = [_l2norm(_conv_silu(pad_ref, 0, dk, cwq_ref, t0, DN_CONV)) * (dk ** -0.5) for t0 in t0s]
        ks = [_l2norm(_conv_silu(pad_ref, dk, dk, cwk_ref, t0, DN_CONV)) for t0 in t0s]
        v2s = [_conv_silu(pad_ref, 2 * dk, 2 * dk, cwv_ref, t0, DN_CONV) for t0 in t0s]
        kbs = [k.astype(BF16) for k in ks]
        kks = [lax.dot_general(kb, kb, _NT, preferred_element_type=F32) for kb in kbs]
        qks = [lax.dot_general(q.astype(BF16), kb, _NT, preferred_element_type=F32) for q, kb in zip(qs, kbs)]
        yield
        gcols = [gcol_ref[pl.ds(t0, c), :] for t0 in t0s]
        chains = [(u, j) for u in range(unroll) for j in range(2)]
        betas = [gcols[u][:, j:j + 1] for u, j in chains]
        gcs = [gcols[u][:, 2 + j:3 + j] for u, j in chains]
        grs = [grow_ref[chunks[u], pl.ds(2 * hq + j, 1), :] for u, j in chains]
        decays = [jnp.exp(jnp.where(causal, gc - gr, NEG_BIG)) for gc, gr in zip(gcs, grs)]
        mats = [jnp.where(strict, beta * kks[u] * decay, 0.0) for (u, j), beta, decay in zip(chains, betas, decays)]
        rs = yield from _unit_lower_inverse_minus_eye(mats)
        egs = [jnp.exp(gc) for gc in gcs]
        rhss = [jnp.concatenate([v2s[u][:, j * dk:(j + 1) * dk] * beta, ks[u] * (beta * eg)], axis=1)
                for (u, j), beta, eg in zip(chains, betas, egs)]
        uws = [(rhs + jnp.dot(r.astype(BF16), rhs.astype(BF16), preferred_element_type=F32)).astype(BF16)
               for r, rhs in zip(rs, rhss)]
        yield
        kds = [(ks[u] * jnp.exp(gc[c - 1:c, :] - gc)).astype(BF16) for (u, j), gc in zip(chains, gcs)]
        cgs = [lax.dot_general(kd, uw, _TN, preferred_element_type=F32) for kd, uw in zip(kds, uws)]
        yield
        oxs = [jnp.dot((qks[u] * decay).astype(BF16), uw, preferred_element_type=F32)
               for (u, j), decay, uw in zip(chains, decays, uws)]
        yield
        for (u, j), cg, ox, eg in zip(chains, cgs, oxs, egs):
            n = chunks[u]
            gq_ref[n, j, 0:c, :] = cg[:, dk:].astype(BF16)
            gq_ref[n, j, c:2 * c, :] = (qs[u] * eg - ox[:, dk:]).astype(BF16)
            cc_ref[n, j] = cg[:, :dk]
            op_ref[n, j] = ox[:, :dk]

    def scan_steps(i):
        for uu in range(unroll):
            n = i * unroll + uu
            t0 = pl.multiple_of(n * c, c)
            z2 = z_ref[pl.ds(t0, c), :].astype(F32)
            for j in range(2):
                s = state_ref[j]
                prod = jnp.dot(gq_ref[n, j], s.astype(BF16), preferred_element_type=F32)
                gr = grow_ref[n, pl.ds(2 * hq + j, 1), :]
                state_ref[j] = s * jnp.exp(gr[:, c - 1:c]) - prod[:c] + cc_ref[n, j]
                o = prod[c:] + op_ref[n, j]
                o = o * lax.rsqrt(jnp.mean(o * o, axis=-1, keepdims=True) + EPS) * nw_ref[...]
                z = z2[:, j * dk:(j + 1) * dk]
                o_ref[pl.ds(t0, c), j * dk:(j + 1) * dk] = (o * (z * jax.nn.sigmoid(z))).astype(o_ref.dtype)
            yield

    def run_interleaved(prep, scan, every):
        for count, _ in enumerate(prep, start=1):
            if count % every == 0:
                next(scan, None)
        for _ in scan:
            pass

    state_ref[...] = jnp.zeros_like(state_ref)
    groups = nch // unroll
    for _ in prep_stages(0):
        pass

    def pipelined_body(i, carry):
        run_interleaved(prep_stages(i), scan_steps(i - 1), every=scan_every)
        return carry

    lax.fori_loop(1, groups, pipelined_body, 0)
    for _ in scan_steps(groups - 1):
        pass


def _deltanet(proj_a, conv_w_t, gcol, grow, norm_w, batch, seq, unroll=4, scan_every=4):
    dk = HEAD_DIM
    nq = DN_QK_HEADS
    c = DN_CHUNK
    nch = seq // c
    return pl.pallas_call(
        functools.partial(_deltanet_kernel, seq=seq, unroll=unroll, scan_every=scan_every),
        grid=(batch, nq),
        in_specs=[pl.BlockSpec((seq, dk), lambda b, h: (b, h)),
                  pl.BlockSpec((seq, dk), lambda b, h: (b, nq + h)),
                  pl.BlockSpec((seq, 2 * dk), lambda b, h: (b, nq + h)),
                  pl.BlockSpec((seq, 2 * dk), lambda b, h: (b, 2 * nq + h)),
                  pl.BlockSpec((DN_CONV, dk), lambda b, h: (0, h)),
                  pl.BlockSpec((DN_CONV, dk), lambda b, h: (0, nq + h)),
                  pl.BlockSpec((DN_CONV, 2 * dk), lambda b, h: (0, nq + h)),
                  pl.BlockSpec((None, None, seq, 4), lambda b, h: (b, h, 0, 0)),
                  pl.BlockSpec((None, nch, DN_V_HEADS, c), lambda b, h: (b, 0, 0, 0)),
                  pl.BlockSpec((1, dk), lambda b, h: (0, 0))],
        out_specs=pl.BlockSpec((seq, 2 * dk), lambda b, h: (b, h)),
        out_shape=jax.ShapeDtypeStruct((batch * seq, A_V), BF16),
        scratch_shapes=[pltpu.VMEM((2, dk, dk), F32),
                        pltpu.VMEM((nch, 2, 2 * c, dk), BF16),
                        pltpu.VMEM((nch, 2, dk, dk), F32),
                        pltpu.VMEM((nch, 2, c, dk), F32),
                        pltpu.VMEM((4, 8 + seq, dk), F32)],
        compiler_params=_params("arbitrary", "arbitrary"),
        name="deltanet",
    )(proj_a, proj_a, proj_a, proj_a, conv_w_t, conv_w_t, conv_w_t, gcol, grow, norm_w.reshape(1, dk))


def _rope_table_kernel(cos_ref, sin_ref):
    shape = cos_ref.shape
    pos = lax.broadcasted_iota(jnp.int32, shape, 0).astype(F32)
    lane = lax.broadcasted_iota(jnp.int32, shape, 1)
    idx = (lane % ROPE_HALF).astype(F32)
    inv = jnp.exp(idx * (-2.0 / ROPE_DIM) * jnp.log(jnp.float32(ROPE_THETA)))
    ang = pos * inv
    cos = jnp.cos(ang)
    sin = jnp.sin(ang)
    cos_ref[...] = jnp.where(lane < ROPE_DIM, cos, 1.0)
    sin_ref[...] = jnp.where(lane < ROPE_HALF, -sin, jnp.where(lane < ROPE_DIM, sin, 0.0))


def _rope_tables(seq):
    shp = jax.ShapeDtypeStruct((seq, HEAD_DIM), F32)
    return pl.pallas_call(_rope_table_kernel, out_shape=[shp, shp], name="rope_tables")()


def _attn_kernel(q0_ref, q1_ref, q2_ref, k_ref, v_ref, z_ref, cos_ref, sin_ref, o_ref,
                 qs_ref, ks_ref, vs_ref, m_ref, l_ref, acc_ref, bias_ref, *, seq, unroll):
    wb = SW_BLOCK
    dh = HEAD_DIM
    rows = 256
    pr = lax.broadcasted_iota(jnp.int32, (dh, dh), 0)
    pc = lax.broadcasted_iota(jnp.int32, (dh, dh), 1)
    swap = jnp.where(((pc < ROPE_HALF) & (pr == pc + ROPE_HALF))
                     | ((pc >= ROPE_HALF) & (pc < ROPE_DIM) & (pr == pc - ROPE_HALF)), 1.0, 0.0).astype(BF16)

    def rope(x, cos, sin):
        swapped = jnp.dot(x.astype(BF16), swap, preferred_element_type=F32)
        return x * cos + swapped * sin

    def rope_body(i, carry):
        r0 = pl.multiple_of(i * rows, rows)
        cos = cos_ref[pl.ds(r0, rows), :]
        sin = sin_ref[pl.ds(r0, rows), :]
        for g, q_ref in enumerate((q0_ref, q1_ref, q2_ref)):
            qs_ref[g, pl.ds(r0, rows), :] = rope(q_ref[pl.ds(r0, rows), :].astype(F32), cos, sin) * (dh ** -0.5)
        ks_ref[pl.ds(r0, rows), :] = rope(k_ref[pl.ds(r0, rows), :].astype(F32), cos, sin)
        vs_ref[pl.ds(r0, rows), :] = v_ref[pl.ds(r0, rows), :].astype(F32)
        return carry

    lax.fori_loop(0, seq // rows, rope_body, 0)

    delta = lax.broadcasted_iota(jnp.int32, (wb, 2 * wb), 1) - lax.broadcasted_iota(jnp.int32, (wb, 2 * wb), 0)
    bias_ref[0] = jnp.where((delta >= 0) & (delta <= wb), 0.0, NEG_BIG)
    bias_ref[1] = jnp.where(delta <= 0, 0.0, NEG_BIG)

    order = sorted(range(len(SW_DILATIONS)), key=lambda gi: -SW_DILATIONS[gi])
    for pos_in_order, g in enumerate(order):
        dil = SW_DILATIONS[g]
        first = pos_in_order == 0
        last = pos_in_order == len(order) - 1
        nb = max(seq // dil // wb, 1)
        has_prev = nb > 1
        kw = 2 * wb if has_prev else wb
        ones = jnp.ones((kw, dh), BF16)

        def block_body(it, carry, g=g, dil=dil, nb=nb, has_prev=has_prev, kw=kw, ones=ones,
                       first=first, last=last):
            def rows_of(ref, s, count):
                if dil == 1:
                    return ref[pl.ds(s, count), :]
                return ref[pl.ds(s, count, stride=dil), :]

            idxs = [it * unroll + u for u in range(unroll)]
            ns = [idx & (nb - 1) for idx in idxs]
            starts = [lax.shift_right_logical(idx, nb.bit_length() - 1) + n * (dil * wb) for idx, n in zip(idxs, ns)]
            if dil == 1:
                starts = [pl.multiple_of(s, wb) for s in starts]
            if has_prev:
                kstarts = [jnp.where(n > 0, s - dil * wb, s) for s, n in zip(starts, ns)]
                biases = [bias_ref[jnp.where(n > 0, 0, 1)] for n in ns]
            else:
                kstarts = starts
                biases = [bias_ref[1, :, 0:wb] for _ in ns]
            qs = [rows_of(qs_ref.at[g], s, wb).astype(BF16) for s in starts]
            kwins = [rows_of(ks_ref, s, kw).astype(BF16) for s in kstarts]
            vaugs = [jnp.concatenate([rows_of(vs_ref, s, kw).astype(BF16), ones], axis=1) for s in kstarts]
            scs = [lax.dot_general(q, kwin, _NT, preferred_element_type=F32) + bias
                   for bias, q, kwin in zip(biases, qs, kwins)]
            m_blks = [jnp.max(sc, axis=-1, keepdims=True) for sc in scs]
            ps = [jnp.exp(sc - m).astype(BF16) for sc, m in zip(scs, m_blks)]
            pvs = [jnp.dot(p, vaug, preferred_element_type=F32) for p, vaug in zip(ps, vaugs)]
            for s, m_blk, pv in zip(starts, m_blks, pvs):
                m_blk = jnp.broadcast_to(m_blk, (wb, dh))
                acc_blk = pv[:, :dh]
                l_blk = pv[:, dh:]
                sl = pl.ds(s, wb) if dil == 1 else pl.ds(s, wb, stride=dil)
                if first:
                    m_ref[sl, :] = m_blk
                    l_ref[sl, :] = l_blk
                    acc_ref[sl, :] = acc_blk
                    continue
                m_old = m_ref[sl, :]
                m_new = jnp.maximum(m_old, m_blk)
                a_old = jnp.exp(m_old - m_new)
                a_blk = jnp.exp(m_blk - m_new)
                l_new = a_old * l_ref[sl, :] + a_blk * l_blk
                acc_new = a_old * acc_ref[sl, :] + a_blk * acc_blk
                if last:
                    z = z_ref[sl, :].astype(F32)
                    o_ref[sl, :] = (acc_new / l_new * (z * jax.nn.sigmoid(z))).astype(o_ref.dtype)
                else:
                    m_ref[sl, :] = m_new
                    l_ref[sl, :] = l_new
                    acc_ref[sl, :] = acc_new
            return carry

        lax.fori_loop(0, dil * nb // unroll, block_body, 0)


def _dilated_attention(proj_b, cos_tab, sin_tab, batch, seq, unroll=8):
    dh = HEAD_DIM
    nh = SW_HEADS
    ng = len(SW_DILATIONS)

    def col(block):
        return pl.BlockSpec((seq, dh), lambda b, h: (b, block * nh + h))

    tab = pl.BlockSpec((seq, dh), lambda b, h: (0, 0))
    return pl.pallas_call(
        functools.partial(_attn_kernel, seq=seq, unroll=unroll),
        grid=(batch, nh),
        in_specs=[col(0), col(1), col(2), col(ng), col(ng + 1), col(ng + 2), tab, tab],
        out_specs=pl.BlockSpec((seq, dh), lambda b, h: (b, h)),
        out_shape=jax.ShapeDtypeStruct((batch * seq, nh * dh), BF16),
        scratch_shapes=[pltpu.VMEM((ng, seq, dh), F32), pltpu.VMEM((seq, dh), F32), pltpu.VMEM((seq, dh), F32),
                        pltpu.VMEM((seq, dh), F32), pltpu.VMEM((seq, dh), F32), pltpu.VMEM((seq, dh), F32),
                        pltpu.VMEM((2, SW_BLOCK, 2 * SW_BLOCK), F32)],
        compiler_params=_params("arbitrary", "arbitrary"),
        name="dilated_attn",
    )(proj_b, proj_b, proj_b, proj_b, proj_b, proj_b, cos_tab, sin_tab)


def _shortconv_kernel(x_ref, wb_ref, wc_ref, wu_ref, wz_ref, cw_ref, o_ref, wq_ref, halo_ref, *, steps_per_seq):
    i = pl.program_id(1)

    @pl.when(i == 0)
    def _():
        for g, w_ref in enumerate((wb_ref, wc_ref, wu_ref, wz_ref)):
            wq_ref[g] = w_ref[...].astype(BF16)

    x = x_ref[...]
    tm = x.shape[0]
    gate_c = jnp.dot(x, wq_ref[1], preferred_element_type=F32)
    u = jnp.dot(x, wq_ref[2], preferred_element_type=F32)
    cu = gate_c * u
    prev = jnp.where(i % steps_per_seq != 0, halo_ref[...], 0.0)
    halo_ref[...] = cu[tm - 8:, :]
    xx = jnp.concatenate([prev, cu], axis=0)
    conv = cw_ref[SC_CONV - 1:SC_CONV, :] * cu
    for t in range(SC_CONV - 1):
        lo = 8 - (SC_CONV - 1) + t
        conv = conv + cw_ref[t:t + 1, :] * xx[lo:lo + tm, :]
    gate_b = jnp.dot(x, wq_ref[0], preferred_element_type=F32)
    z = jnp.dot(x, wq_ref[3], preferred_element_type=F32)
    o_ref[...] = (gate_b * conv * (z * jax.nn.sigmoid(z))).astype(o_ref.dtype)


def _shortconv_mix(hn, w_in, layer, conv_w_t, seq, tm=1024, tn=256):
    m, k = hn.shape
    nblk = SC_WIDTH // tn

    def wspec(g):
        return pl.BlockSpec((None, k, tn), lambda j, i: (layer, 0, g * nblk + j))

    return pl.pallas_call(
        functools.partial(_shortconv_kernel, steps_per_seq=seq // tm),
        grid=(nblk, m // tm),
        in_specs=[pl.BlockSpec((tm, k), lambda j, i: (i, 0)), wspec(0), wspec(1), wspec(2), wspec(3),
                  pl.BlockSpec((None, SC_CONV, tn), lambda j, i: (layer, 0, j))],
        out_specs=pl.BlockSpec((tm, tn), lambda j, i: (i, j)),
        out_shape=jax.ShapeDtypeStruct((m, SC_WIDTH), BF16),
        scratch_shapes=[pltpu.VMEM((4, k, tn), BF16), pltpu.VMEM((8, tn), F32)],
        compiler_params=_params("arbitrary", "arbitrary"),
        name="shortconv_mix",
    )(hn, w_in, w_in, w_in, w_in, conv_w_t)


def kernel(x, norm_w, hyb_w_in, dn_conv_w, dn_a_log, dn_dt_bias, dn_norm_w, hyb_w_out, sc_w_in, sc_conv_w,
           sc_w_out, final_norm_w):
    batch, seq, d = x.shape
    h = x.reshape(batch * seq, d)
    cos_tab, sin_tab = _rope_tables(seq)
    dn_conv_t = jnp.swapaxes(dn_conv_w, 1, 2)
    sc_conv_t = jnp.swapaxes(sc_conv_w, 1, 2)
    hyb_wt = jnp.swapaxes(hyb_w_in, 1, 2)
    hn = _rmsnorm(h, norm_w[0], BF16)
    for layer in range(DEPTH):
        li = layer // 2
        last = layer == DEPTH - 1
        next_norm = final_norm_w if last else norm_w[layer + 1]
        if layer % 2 == 0:
            proj_a = _matmul_nt(hn, hyb_wt, li, 0, A_WIDTH, BF16)
            proj_b = _matmul_nt(hn, hyb_wt, li, B_OFF, B_WIDTH, BF16)
            gcol, grow = _gates(hn, hyb_wt, li, dn_a_log[li], dn_dt_bias[li], batch, seq)
            y_a = _deltanet(proj_a, dn_conv_t[li], gcol, grow, dn_norm_w[li], batch, seq)
            y_b = _dilated_attention(proj_b, cos_tab, sin_tab, batch, seq)
            outs = _out_proj_residual_norm([y_a, y_b], _cast_bf16(hyb_w_out, li), h, next_norm, last)
        else:
            y = _shortconv_mix(hn, sc_w_in, li, sc_conv_t, seq)
            outs = _out_proj_residual_norm([y], _cast_bf16(sc_w_out, li), h, next_norm, last)
        if last:
            return outs[0].reshape(batch, seq, d)
        h, hn = outs
```

```python
import functools

import jax
import jax.numpy as jnp
from jax import lax
from jax.experimental import pallas as pl
from jax.experimental.pallas import tpu as pltpu

F32 = jnp.float32
BF16 = jnp.bfloat16

D_MODEL = 2048
DEPTH = 4
HEAD_DIM = 128
EPS = 1e-6
DN_QK_HEADS = 8
DN_V_HEADS = 16
DN_CONV = 4
DN_CHUNK = 128
SW_HEADS = 8
SW_DILATIONS = (1, 4, 16)
SW_BLOCK = 128
ROPE_THETA = 500000.0
ROPE_DIM = HEAD_DIM // 4
ROPE_HALF = ROPE_DIM // 2
SC_WIDTH = 3072
SC_CONV = 3

A_Q = DN_QK_HEADS * HEAD_DIM
A_V = DN_V_HEADS * HEAD_DIM
A_WIDTH = 2 * A_Q + 2 * A_V
GATE_OFF = A_WIDTH
B_OFF = A_WIDTH + 2 * DN_V_HEADS
B_WIDTH = (len(SW_DILATIONS) + 3) * SW_HEADS * HEAD_DIM
HYB_MIX = A_V + SW_HEADS * HEAD_DIM

LANES = 128
VMEM_LIMIT = 56 * 1024 * 1024
NEG_BIG = -1e30

_NT = (((1,), (1,)), ((), ()))
_TN = (((0,), (0,)), ((), ()))


def _params(*sem):
    return pltpu.CompilerParams(dimension_semantics=sem, vmem_limit_bytes=VMEM_LIMIT)


def _rmsnorm_kernel(x_ref, w_ref, o_ref):
    x = x_ref[...]
    ms = jnp.mean(x * x, axis=-1, keepdims=True)
    o_ref[...] = (x * lax.rsqrt(ms + EPS) * w_ref[...]).astype(o_ref.dtype)


def _rmsnorm(x, w, out_dtype, tm=512):
    m, d = x.shape
    return pl.pallas_call(
        _rmsnorm_kernel,
        grid=(m // tm,),
        in_specs=[pl.BlockSpec((tm, d), lambda i: (i, 0)),
                  pl.BlockSpec((1, d), lambda i: (0, 0))],
        out_specs=pl.BlockSpec((tm, d), lambda i: (i, 0)),
        out_shape=jax.ShapeDtypeStruct((m, d), out_dtype),
        compiler_params=_params("arbitrary"),
        name="rmsnorm",
    )(x, w.reshape(1, d))


def _mm_nt_kernel(x_ref, w_ref, wx_ref, o_ref, wb_ref, *, shift):
    @pl.when(pl.program_id(1) == 0)
    def _():
        tn = wb_ref.shape[0]
        wb_ref[0:tn - shift, :] = w_ref[shift:tn, :].astype(BF16)
        if shift:
            wb_ref[tn - shift:tn, :] = wx_ref[...].astype(BF16)

    o_ref[...] = lax.dot_general(x_ref[...], wb_ref[...], _NT, preferred_element_type=F32).astype(o_ref.dtype)


def _matmul_nt(x, wt, layer, row0, n_rows, out_dtype, tm=1024, tn=1024, shift_rows=32):
    m, k = x.shape
    shift = row0 % tn
    assert shift in (0, shift_rows) and tn % shift_rows == 0
    base = row0 // tn
    return pl.pallas_call(
        functools.partial(_mm_nt_kernel, shift=shift),
        grid=(n_rows // tn, m // tm),
        in_specs=[pl.BlockSpec((tm, k), lambda j, i: (i, 0)),
                  pl.BlockSpec((None, tn, k), lambda j, i: (layer, base + j, 0)),
                  pl.BlockSpec((None, shift_rows, k), lambda j, i: (layer, (base + j + 1) * (tn // shift_rows), 0))],
        out_specs=pl.BlockSpec((tm, tn), lambda j, i: (i, j)),
        out_shape=jax.ShapeDtypeStruct((m, n_rows), out_dtype),
        scratch_shapes=[pltpu.VMEM((tn, k), BF16)],
        compiler_params=_params("arbitrary", "arbitrary"),
        name="in_proj",
    )(x, wt, wt)


def _cast_kernel(x_ref, o_ref):
    o_ref[...] = x_ref[...].astype(o_ref.dtype)


def _cast_bf16(w, layer, rows=512):
    _, k, n = w.shape
    return pl.pallas_call(
        _cast_kernel,
        grid=(k // rows,),
        in_specs=[pl.BlockSpec((None, rows, n), lambda i: (layer, i, 0))],
        out_specs=pl.BlockSpec((rows, n), lambda i: (i, 0)),
        out_shape=jax.ShapeDtypeStruct((k, n), BF16),
        compiler_params=_params("arbitrary"),
        name="cast_bf16",
    )(w)


def _out_proj_kernel(*refs, k_sizes, emit_h):
    nx = len(k_sizes)
    x_refs = refs[:nx]
    w_ref, h_ref, nw_ref = refs[nx:nx + 3]
    out_refs = refs[nx + 3:]
    acc = h_ref[...]
    off = 0
    for x_ref, ks in zip(x_refs, k_sizes):
        acc = acc + jnp.dot(x_ref[...], w_ref[off:off + ks, :], preferred_element_type=F32)
        off += ks
    if emit_h:
        out_refs[0][...] = acc
    hn_ref = out_refs[-1]
    ms = jnp.mean(acc * acc, axis=-1, keepdims=True)
    hn_ref[...] = (acc * lax.rsqrt(ms + EPS) * nw_ref[...]).astype(hn_ref.dtype)


def _out_proj_residual_norm(xs, w_bf16, h, norm_w, last, tm=512):
    m, n = h.shape
    k_sizes = tuple(x.shape[1] for x in xs)
    k = sum(k_sizes)
    row = pl.BlockSpec((tm, n), lambda i: (i, 0))
    if last:
        out_shape = [jax.ShapeDtypeStruct((m, n), F32)]
    else:
        out_shape = [jax.ShapeDtypeStruct((m, n), F32), jax.ShapeDtypeStruct((m, n), BF16)]
    outs = pl.pallas_call(
        functools.partial(_out_proj_kernel, k_sizes=k_sizes, emit_h=not last),
        grid=(m // tm,),
        in_specs=[pl.BlockSpec((tm, ks), lambda i: (i, 0)) for ks in k_sizes]
        + [pl.BlockSpec((k, n), lambda i: (0, 0), pipeline_mode=pl.Buffered(1)), row,
           pl.BlockSpec((1, n), lambda i: (0, 0))],
        out_specs=[row] * len(out_shape),
        out_shape=out_shape,
        compiler_params=_params("arbitrary"),
        name="out_proj",
    )(*xs, w_bf16, h, norm_w.reshape(1, n))
    return outs


def _softplus(x):
    return jnp.maximum(x, 0.0) + jnp.log1p(jnp.exp(-jnp.abs(x)))


def _gates_kernel(x_ref, w_ref, alog_ref, dtb_ref, alog_c_ref, dtb_c_ref, col_ref, row_ref, ws_ref, *, tm):
    nh = DN_V_HEADS
    c = DN_CHUNK

    @pl.when(pl.program_id(0) == 0)
    def _():
        ws_ref[...] = w_ref[...].astype(BF16)

    x = x_ref[...]
    ri = lax.broadcasted_iota(jnp.int32, (c, c), 0)
    ci = lax.broadcasted_iota(jnp.int32, (c, c), 1)
    lower = jnp.where(ci <= ri, 1.0, 0.0).astype(BF16)
    upper = jnp.where(ri <= ci, 1.0, 0.0).astype(BF16)

    def split3(v):
        hi = v.astype(BF16)
        r1 = v - hi.astype(F32)
        mid = r1.astype(BF16)
        lo = (r1 - mid.astype(F32)).astype(BF16)
        return hi, mid, lo

    p = lax.dot_general(x, ws_ref[...], _NT, preferred_element_type=F32)
    beta = jax.nn.sigmoid(p[:, :nh])
    g = -jnp.exp(alog_ref[...]) * _softplus(p[:, nh:2 * nh] + dtb_ref[...])
    pt = lax.dot_general(ws_ref[...], x, _NT, preferred_element_type=F32)
    gt = -jnp.exp(alog_c_ref[...]) * _softplus(pt[nh:2 * nh, :] + dtb_c_ref[...])
    g_parts = split3(g)
    gt_parts = split3(gt)
    for ch in range(tm // c):
        rows = slice(ch * c, (ch + 1) * c)
        gc = sum(jnp.dot(lower, part[rows, :], preferred_element_type=F32) for part in g_parts)
        gct = sum(jnp.dot(part[:, rows], upper, preferred_element_type=F32) for part in gt_parts)
        for hq in range(DN_QK_HEADS):
            col_ref[hq, rows, 0:2] = beta[rows, 2 * hq:2 * hq + 2]
            col_ref[hq, rows, 2:4] = gc[:, 2 * hq:2 * hq + 2]
        row_ref[ch] = gct


def _gates(hn, wt, layer, a_log, dt_bias, batch, seq, tm=512):
    m, k = hn.shape
    nh = DN_V_HEADS
    nch = seq // DN_CHUNK
    ng = 2 * nh
    assert GATE_OFF % ng == 0
    steps = seq // tm
    col, row = pl.pallas_call(
        functools.partial(_gates_kernel, tm=tm),
        grid=(m // tm,),
        in_specs=[pl.BlockSpec((tm, k), lambda i: (i, 0)),
                  pl.BlockSpec((None, ng, k), lambda i: (layer, GATE_OFF // ng, 0)),
                  pl.BlockSpec((1, nh), lambda i: (0, 0)),
                  pl.BlockSpec((1, nh), lambda i: (0, 0)),
                  pl.BlockSpec((nh, 1), lambda i: (0, 0)),
                  pl.BlockSpec((nh, 1), lambda i: (0, 0))],
        out_specs=[pl.BlockSpec((None, DN_QK_HEADS, tm, 4), lambda i: (i // steps, 0, i % steps, 0)),
                   pl.BlockSpec((None, tm // DN_CHUNK, nh, DN_CHUNK), lambda i: (i // steps, i % steps, 0, 0))],
        out_shape=[jax.ShapeDtypeStruct((batch, DN_QK_HEADS, seq, 4), F32),
                   jax.ShapeDtypeStruct((batch, nch, nh, DN_CHUNK), F32)],
        scratch_shapes=[pltpu.VMEM((ng, k), BF16)],
        compiler_params=_params("arbitrary"),
        name="dn_gates",
    )(hn, wt, a_log.reshape(1, nh), dt_bias.reshape(1, nh), a_log.reshape(nh, 1), dt_bias.reshape(nh, 1))
    return col, row


def _conv_silu(pad_ref, col0, ncols, cw_ref, t0, width):
    c = DN_CHUNK
    acc = None
    for i in range(width):
        parts = [pad_ref[col0 // HEAD_DIM + p, pl.ds(t0 + (8 - (width - 1) + i), c, stride=1), :]
                 for p in range(ncols // HEAD_DIM)]
        x = parts[0] if len(parts) == 1 else jnp.concatenate(parts, axis=1)
        term = cw_ref[i:i + 1, :] * x
        acc = term if acc is None else acc + term
    return _silu(acc)


def _l2norm(x):
    return x * lax.rsqrt(jnp.sum(x * x, axis=-1, keepdims=True) + EPS)


def _silu(x):
    return x / (1.0 + jnp.exp(-x))


def _unit_lower_inverse(mats):
    c = mats[0].shape[0]
    ri = lax.broadcasted_iota(jnp.int32, (c, c), 0)
    ci = lax.broadcasted_iota(jnp.int32, (c, c), 1)
    x = ri ^ ci
    zero = jnp.zeros((), BF16)
    eye = jnp.where(ri == ci, 1.0, 0.0).astype(BF16)
    ts = [eye - jnp.where(x < 2, a, zero) for a in mats]
    b = 2
    while b < c:
        level = (x >= b) & (x < 2 * b)
        ls = [jnp.where(level, a, zero) for a in mats]
        xs = [jnp.dot(t, l, preferred_element_type=F32).astype(BF16) for t, l in zip(ts, ls)]
        yield
        ts = [t - jnp.dot(xm, t, preferred_element_type=F32).astype(BF16) for t, xm in zip(ts, xs)]
        yield
        b *= 2
    return ts


def _deltanet_kernel(q_ref, k_ref, v_ref, z_ref, cwq_ref, cwk_ref, cwv_ref, gcol_ref, grow_ref, nw_ref,
                     o_ref, state_ref, gq_ref, cc_ref, op_ref, pad_ref, *, seq, unroll, scan_every):
    c = DN_CHUNK
    dk = HEAD_DIM
    nch = seq // c
    hq = pl.program_id(1)
    ri = lax.broadcasted_iota(jnp.int32, (c, c), 0)
    ci = lax.broadcasted_iota(jnp.int32, (c, c), 1)
    causal = ri >= ci
    strict = ri > ci

    pad_ref[:, 0:8, :] = jnp.zeros((4, 8, dk), F32)

    def pad_body(i, carry):
        r0 = pl.multiple_of(i * c, c)
        pad_ref[0, pl.ds(8 + r0, c), :] = q_ref[pl.ds(r0, c), :].astype(F32)
        pad_ref[1, pl.ds(8 + r0, c), :] = k_ref[pl.ds(r0, c), :].astype(F32)
        pad_ref[2, pl.ds(8 + r0, c), :] = v_ref[pl.ds(r0, c), 0:dk].astype(F32)
        pad_ref[3, pl.ds(8 + r0, c), :] = v_ref[pl.ds(r0, c), dk:2 * dk].astype(F32)
        return carry

    lax.fori_loop(0, nch, pad_body, 0)

    def prep_stages(i):
        chunks = [i * unroll + uu for uu in range(unroll)]
        t0s = [pl.multiple_of(n * c, c) for n in chunks]
        qs = [_l2norm(_conv_silu(pad_ref, 0, dk, cwq_ref, t0, DN_CONV)) * (dk ** -0.5) for t0 in t0s]
        ks = [_l2norm(_conv_silu(pad_ref, dk, dk, cwk_ref, t0, DN_CONV)) for t0 in t0s]
        v2s = [_conv_silu(pad_ref, 2 * dk, 2 * dk, cwv_ref, t0, DN_CONV) for t0 in t0s]
        kbs = [k.astype(BF16) for k in ks]
        kks = [lax.dot_general(kb, kb, _NT, preferred_element_type=F32) for kb in kbs]
        qks = [lax.dot_general(q.astype(BF16), kb, _NT, preferred_element_type=F32) for q, kb in zip(qs, kbs)]
        yield
        gcols = [gcol_ref[pl.ds(t0, c), :] for t0 in t0s]
        chains = [(u, j) for u in range(unroll) for j in range(2)]
        betas = [gcols[u][:, j:j + 1] for u, j in chains]
        gcs = [gcols[u][:, 2 + j:3 + j] for u, j in chains]
        grs = [grow_ref[chunks[u], pl.ds(2 * hq + j, 1), :] for u, j in chains]
        decays = [jnp.exp(jnp.where(causal, gc - gr, NEG_BIG)) for gc, gr in zip(gcs, grs)]
        mats = [jnp.where(strict, beta * kks[u] * decay, 0.0).astype(BF16)
                for (u, j), beta, decay in zip(chains, betas, decays)]
        ts = yield from _unit_lower_inverse(mats)
        egs = [jnp.exp(gc) for gc in gcs]
        rhss = [jnp.concatenate([v2s[u][:, j * dk:(j + 1) * dk] * beta, ks[u] * (beta * eg)], axis=1)
                for (u, j), beta, eg in zip(chains, betas, egs)]
        uws = [jnp.dot(t, rhs.astype(BF16), preferred_element_type=F32).astype(BF16)
               for t, rhs in zip(ts, rhss)]
        yield
        kds = [(ks[u] * jnp.exp(gc[c - 1:c, :] - gc)).astype(BF16) for (u, j), gc in zip(chains, gcs)]
        cgs = [lax.dot_general(kd, uw, _TN, preferred_element_type=F32) for kd, uw in zip(kds, uws)]
        yield
        oxs = [jnp.dot((qks[u] * decay).astype(BF16), uw, preferred_element_type=F32)
               for (u, j), decay, uw in zip(chains, decays, uws)]
        yield
        for (u, j), cg, ox, eg in zip(chains, cgs, oxs, egs):
            n = chunks[u]
            gq_ref[n, j, 0:c, :] = cg[:, dk:].astype(BF16)
            gq_ref[n, j, c:2 * c, :] = (qs[u] * eg - ox[:, dk:]).astype(BF16)
            cc_ref[n, j] = cg[:, :dk]
            op_ref[n, j] = ox[:, :dk]

    def scan_steps(i):
        for uu in range(unroll):
            n = i * unroll + uu
            t0 = pl.multiple_of(n * c, c)
            z2 = z_ref[pl.ds(t0, c), :].astype(F32)
            for j in range(2):
                s = state_ref[j]
                prod = jnp.dot(gq_ref[n, j], s.astype(BF16), preferred_element_type=F32)
                gr = grow_ref[n, pl.ds(2 * hq + j, 1), :]
                state_ref[j] = s * jnp.exp(gr[:, c - 1:c]) - prod[:c] + cc_ref[n, j]
                o = prod[c:] + op_ref[n, j]
                o = o * lax.rsqrt(jnp.mean(o * o, axis=-1, keepdims=True) + EPS) * nw_ref[...]
                z = z2[:, j * dk:(j + 1) * dk]
                o_ref[pl.ds(t0, c), j * dk:(j + 1) * dk] = (o * _silu(z)).astype(o_ref.dtype)
            yield

    def run_interleaved(prep, scan, every):
        for count, _ in enumerate(prep, start=1):
            if count % every == 0:
                next(scan, None)
        for _ in scan:
            pass

    state_ref[...] = jnp.zeros_like(state_ref)
    groups = nch // unroll
    for _ in prep_stages(0):
        pass

    def pipelined_body(i, carry):
        run_interleaved(prep_stages(i), scan_steps(i - 1), every=scan_every)
        return carry

    lax.fori_loop(1, groups, pipelined_body, 0)
    for _ in scan_steps(groups - 1):
        pass


def _deltanet(proj_a, conv_w_t, gcol, grow, norm_w, batch, seq, unroll=4, scan_every=3):
    dk = HEAD_DIM
    nq = DN_QK_HEADS
    c = DN_CHUNK
    nch = seq // c
    return pl.pallas_call(
        functools.partial(_deltanet_kernel, seq=seq, unroll=unroll, scan_every=scan_every),
        grid=(batch, nq),
        in_specs=[pl.BlockSpec((seq, dk), lambda b, h: (b, h)),
                  pl.BlockSpec((seq, dk), lambda b, h: (b, nq + h)),
                  pl.BlockSpec((seq, 2 * dk), lambda b, h: (b, nq + h)),
                  pl.BlockSpec((seq, 2 * dk), lambda b, h: (b, 2 * nq + h)),
                  pl.BlockSpec((DN_CONV, dk), lambda b, h: (0, h)),
                  pl.BlockSpec((DN_CONV, dk), lambda b, h: (0, nq + h)),
                  pl.BlockSpec((DN_CONV, 2 * dk), lambda b, h: (0, nq + h)),
                  pl.BlockSpec((None, None, seq, 4), lambda b, h: (b, h, 0, 0)),
                  pl.BlockSpec((None, nch, DN_V_HEADS, c), lambda b, h: (b, 0, 0, 0)),
                  pl.BlockSpec((1, dk), lambda b, h: (0, 0))],
        out_specs=pl.BlockSpec((seq, 2 * dk), lambda b, h: (b, h)),
        out_shape=jax.ShapeDtypeStruct((batch * seq, A_V), BF16),
        scratch_shapes=[pltpu.VMEM((2, dk, dk), F32),
                        pltpu.VMEM((nch, 2, 2 * c, dk), BF16),
                        pltpu.VMEM((nch, 2, dk, dk), F32),
                        pltpu.VMEM((nch, 2, c, dk), F32),
                        pltpu.VMEM((4, 8 + seq, dk), F32)],
        compiler_params=_params("arbitrary", "arbitrary"),
        name="deltanet",
    )(proj_a, proj_a, proj_a, proj_a, conv_w_t, conv_w_t, conv_w_t, gcol, grow, norm_w.reshape(1, dk))


def _rope_table_kernel(cos_ref, sin_ref):
    shape = cos_ref.shape
    pos = lax.broadcasted_iota(jnp.int32, shape, 0).astype(F32)
    lane = lax.broadcasted_iota(jnp.int32, shape, 1)
    idx = (lane % ROPE_HALF).astype(F32)
    inv = jnp.exp(idx * (-2.0 / ROPE_DIM) * jnp.log(jnp.float32(ROPE_THETA)))
    ang = pos * inv
    cos = jnp.cos(ang)
    sin = jnp.sin(ang)
    cos_ref[...] = jnp.where(lane < ROPE_DIM, cos, 1.0)
    sin_ref[...] = jnp.where(lane < ROPE_HALF, -sin, jnp.where(lane < ROPE_DIM, sin, 0.0))


def _rope_tables(seq):
    shp = jax.ShapeDtypeStruct((seq, HEAD_DIM), F32)
    return pl.pallas_call(_rope_table_kernel, out_shape=[shp, shp], name="rope_tables")()


def _attn_kernel(q0_ref, q1_ref, q2_ref, k_ref, v_ref, z_ref, cos_ref, sin_ref, o_ref,
                 qs_ref, ks_ref, vs_ref, m_ref, l_ref, acc_ref, bias_ref, *, seq, unroll):
    wb = SW_BLOCK
    dh = HEAD_DIM
    rows = 256
    pr = lax.broadcasted_iota(jnp.int32, (dh, dh), 0)
    pc = lax.broadcasted_iota(jnp.int32, (dh, dh), 1)
    swap = jnp.where(((pc < ROPE_HALF) & (pr == pc + ROPE_HALF))
                     | ((pc >= ROPE_HALF) & (pc < ROPE_DIM) & (pr == pc - ROPE_HALF)), 1.0, 0.0).astype(BF16)

    def rope(x, cos, sin):
        swapped = jnp.dot(x.astype(BF16), swap, preferred_element_type=F32)
        return x * cos + swapped * sin

    def rope_body(i, carry):
        r0 = pl.multiple_of(i * rows, rows)
        cos = cos_ref[pl.ds(r0, rows), :]
        sin = sin_ref[pl.ds(r0, rows), :]
        for g, q_ref in enumerate((q0_ref, q1_ref, q2_ref)):
            qs_ref[g, pl.ds(r0, rows), :] = rope(q_ref[pl.ds(r0, rows), :].astype(F32), cos, sin) * (dh ** -0.5)
        ks_ref[pl.ds(r0, rows), :] = rope(k_ref[pl.ds(r0, rows), :].astype(F32), cos, sin)
        vs_ref[pl.ds(r0, rows), :] = v_ref[pl.ds(r0, rows), :].astype(F32)
        return carry

    lax.fori_loop(0, seq // rows, rope_body, 0)

    delta = lax.broadcasted_iota(jnp.int32, (wb, 2 * wb), 1) - lax.broadcasted_iota(jnp.int32, (wb, 2 * wb), 0)
    bias_ref[0] = jnp.where((delta >= 0) & (delta <= wb), 0.0, NEG_BIG)
    bias_ref[1] = jnp.where(delta <= 0, 0.0, NEG_BIG)

    order = sorted(range(len(SW_DILATIONS)), key=lambda gi: -SW_DILATIONS[gi])
    for pos_in_order, g in enumerate(order):
        dil = SW_DILATIONS[g]
        first = pos_in_order == 0
        last = pos_in_order == len(order) - 1
        nb = max(seq // dil // wb, 1)
        has_prev = nb > 1
        kw = 2 * wb if has_prev else wb
        ones = jnp.ones((kw, dh), BF16)

        def block_body(it, carry, g=g, dil=dil, nb=nb, has_prev=has_prev, kw=kw, ones=ones,
                       first=first, last=last):
            def rows_of(ref, s, count):
                if dil == 1:
                    return ref[pl.ds(s, count), :]
                return ref[pl.ds(s, count, stride=dil), :]

            idxs = [it * unroll + u for u in range(unroll)]
            ns = [idx & (nb - 1) for idx in idxs]
            starts = [lax.shift_right_logical(idx, nb.bit_length() - 1) + n * (dil * wb) for idx, n in zip(idxs, ns)]
            if dil == 1:
                starts = [pl.multiple_of(s, wb) for s in starts]
            if has_prev:
                kstarts = [jnp.where(n > 0, s - dil * wb, s) for s, n in zip(starts, ns)]
                biases = [bias_ref[jnp.where(n > 0, 0, 1)] for n in ns]
            else:
                kstarts = starts
                biases = [bias_ref[1, :, 0:wb] for _ in ns]
            qs = [rows_of(qs_ref.at[g], s, wb).astype(BF16) for s in starts]
            kwins = [rows_of(ks_ref, s, kw).astype(BF16) for s in kstarts]
            vaugs = [jnp.concatenate([rows_of(vs_ref, s, kw).astype(BF16), ones], axis=1) for s in kstarts]
            scs = [lax.dot_general(q, kwin, _NT, preferred_element_type=F32) + bias
                   for bias, q, kwin in zip(biases, qs, kwins)]
            m_blks = [jnp.max(sc, axis=-1, keepdims=True) for sc in scs]
            ps = [jnp.exp(sc - m).astype(BF16) for sc, m in zip(scs, m_blks)]
            pvs = [jnp.dot(p, vaug, preferred_element_type=F32) for p, vaug in zip(ps, vaugs)]
            for s, m_blk, pv in zip(starts, m_blks, pvs):
                m_blk = jnp.broadcast_to(m_blk, (wb, dh))
                acc_blk = pv[:, :dh]
                l_blk = pv[:, dh:]
                sl = pl.ds(s, wb) if dil == 1 else pl.ds(s, wb, stride=dil)
                if first:
                    m_ref[sl, :] = m_blk
                    l_ref[sl, :] = l_blk
                    acc_ref[sl, :] = acc_blk
                    continue
                m_old = m_ref[sl, :]
                m_new = jnp.maximum(m_old, m_blk)
                a_old = jnp.exp(m_old - m_new)
                a_blk = jnp.exp(m_blk - m_new)
                l_new = a_old * l_ref[sl, :] + a_blk * l_blk
                acc_new = a_old * acc_ref[sl, :] + a_blk * acc_blk
                if last:
                    z = z_ref[sl, :].astype(F32)
                    o_ref[sl, :] = (acc_new / l_new * _silu(z)).astype(o_ref.dtype)
                else:
                    m_ref[sl, :] = m_new
                    l_ref[sl, :] = l_new
                    acc_ref[sl, :] = acc_new
            return carry

        lax.fori_loop(0, dil * nb // unroll, block_body, 0)


def _dilated_attention(proj_b, cos_tab, sin_tab, batch, seq, unroll=8):
    dh = HEAD_DIM
    nh = SW_HEADS
    ng = len(SW_DILATIONS)

    def col(block):
        return pl.BlockSpec((seq, dh), lambda b, h: (b, block * nh + h))

    tab = pl.BlockSpec((seq, dh), lambda b, h: (0, 0))
    return pl.pallas_call(
        functools.partial(_attn_kernel, seq=seq, unroll=unroll),
        grid=(batch, nh),
        in_specs=[col(0), col(1), col(2), col(ng), col(ng + 1), col(ng + 2), tab, tab],
        out_specs=pl.BlockSpec((seq, dh), lambda b, h: (b, h)),
        out_shape=jax.ShapeDtypeStruct((batch * seq, nh * dh), BF16),
        scratch_shapes=[pltpu.VMEM((ng, seq, dh), F32), pltpu.VMEM((seq, dh), F32), pltpu.VMEM((seq, dh), F32),
                        pltpu.VMEM((seq, dh), F32), pltpu.VMEM((seq, dh), F32), pltpu.VMEM((seq, dh), F32),
                        pltpu.VMEM((2, SW_BLOCK, 2 * SW_BLOCK), F32)],
        compiler_params=_params("arbitrary", "arbitrary"),
        name="dilated_attn",
    )(proj_b, proj_b, proj_b, proj_b, proj_b, proj_b, cos_tab, sin_tab)


def _shortconv_kernel(x_ref, wb_ref, wc_ref, wu_ref, wz_ref, cw_ref, o_ref, wq_ref, halo_ref, *, steps_per_seq):
    i = pl.program_id(1)

    @pl.when(i == 0)
    def _():
        for g, w_ref in enumerate((wb_ref, wc_ref, wu_ref, wz_ref)):
            wq_ref[g] = w_ref[...].astype(BF16)

    x = x_ref[...]
    tm = x.shape[0]
    gate_c = jnp.dot(x, wq_ref[1], preferred_element_type=F32)
    u = jnp.dot(x, wq_ref[2], preferred_element_type=F32)
    cu = gate_c * u
    prev = jnp.where(i % steps_per_seq != 0, halo_ref[...], 0.0)
    halo_ref[...] = cu[tm - 8:, :]
    xx = jnp.concatenate([prev, cu], axis=0)
    conv = cw_ref[SC_CONV - 1:SC_CONV, :] * cu
    for t in range(SC_CONV - 1):
        lo = 8 - (SC_CONV - 1) + t
        conv = conv + cw_ref[t:t + 1, :] * xx[lo:lo + tm, :]
    gate_b = jnp.dot(x, wq_ref[0], preferred_element_type=F32)
    z = jnp.dot(x, wq_ref[3], preferred_element_type=F32)
    o_ref[...] = (gate_b * conv * _silu(z)).astype(o_ref.dtype)


def _shortconv_mix(hn, w_in, layer, conv_w_t, seq, tm=1024, tn=256):
    m, k = hn.shape
    nblk = SC_WIDTH // tn

    def wspec(g):
        return pl.BlockSpec((None, k, tn), lambda j, i: (layer, 0, g * nblk + j))

    return pl.pallas_call(
        functools.partial(_shortconv_kernel, steps_per_seq=seq // tm),
        grid=(nblk, m // tm),
        in_specs=[pl.BlockSpec((tm, k), lambda j, i: (i, 0)), wspec(0), wspec(1), wspec(2), wspec(3),
                  pl.BlockSpec((None, SC_CONV, tn), lambda j, i: (layer, 0, j))],
        out_specs=pl.BlockSpec((tm, tn), lambda j, i: (i, j)),
        out_shape=jax.ShapeDtypeStruct((m, SC_WIDTH), BF16),
        scratch_shapes=[pltpu.VMEM((4, k, tn), BF16), pltpu.VMEM((8, tn), F32)],
        compiler_params=_params("arbitrary", "arbitrary"),
        name="shortconv_mix",
    )(hn, w_in, w_in, w_in, w_in, conv_w_t)


def kernel(x, norm_w, hyb_w_in, dn_conv_w, dn_a_log, dn_dt_bias, dn_norm_w, hyb_w_out, sc_w_in, sc_conv_w,
           sc_w_out, final_norm_w):
    batch, seq, d = x.shape
    h = x.reshape(batch * seq, d)
    cos_tab, sin_tab = _rope_tables(seq)
    dn_conv_t = jnp.swapaxes(dn_conv_w, 1, 2)
    sc_conv_t = jnp.swapaxes(sc_conv_w, 1, 2)
    hyb_wt = jnp.swapaxes(hyb_w_in, 1, 2)
    hn = _rmsnorm(h, norm_w[0], BF16)
    for layer in range(DEPTH):
        li = layer // 2
        last = layer == DEPTH - 1
        next_norm = final_norm_w if last else norm_w[layer + 1]
        if layer % 2 == 0:
            proj_a = _matmul_nt(hn, hyb_wt, li, 0, A_WIDTH, BF16)
            proj_b = _matmul_nt(hn, hyb_wt, li, B_OFF, B_WIDTH, BF16)
            gcol, grow = _gates(hn, hyb_wt, li, dn_a_log[li], dn_dt_bias[li], batch, seq)
            y_a = _deltanet(proj_a, dn_conv_t[li], gcol, grow, dn_norm_w[li], batch, seq)
            y_b = _dilated_attention(proj_b, cos_tab, sin_tab, batch, seq)
            outs = _out_proj_residual_norm([y_a, y_b], _cast_bf16(hyb_w_out, li), h, next_norm, last)
        else:
            y = _shortconv_mix(hn, sc_w_in, li, sc_conv_t, seq)
            outs = _out_proj_residual_norm([y], _cast_bf16(sc_w_out, li), h, next_norm, last)
        if last:
            return outs[0].reshape(batch, seq, d)
        h, hn = outs
```

```python
import functools

import jax
import jax.numpy as jnp
from jax import lax
from jax.experimental import pallas as pl
from jax.experimental.pallas import tpu as pltpu

F32 = jnp.float32
BF16 = jnp.bfloat16

D_MODEL = 2048
DEPTH = 4
HEAD_DIM = 128
EPS = 1e-6
DN_QK_HEADS = 8
DN_V_HEADS = 16
DN_CONV = 4
DN_CHUNK = 128
SW_HEADS = 8
SW_DILATIONS = (1, 4, 16)
SW_BLOCK = 128
ROPE_THETA = 500000.0
ROPE_DIM = HEAD_DIM // 4
ROPE_HALF = ROPE_DIM // 2
SC_WIDTH = 3072
SC_CONV = 3

A_Q = DN_QK_HEADS * HEAD_DIM
A_V = DN_V_HEADS * HEAD_DIM
A_WIDTH = 2 * A_Q + 2 * A_V
GATE_OFF = A_WIDTH
B_OFF = A_WIDTH + 2 * DN_V_HEADS
B_WIDTH = (len(SW_DILATIONS) + 3) * SW_HEADS * HEAD_DIM
HYB_MIX = A_V + SW_HEADS * HEAD_DIM

LANES = 128
VMEM_LIMIT = 56 * 1024 * 1024
NEG_BIG = -1e30

_NT = (((1,), (1,)), ((), ()))
_TN = (((0,), (0,)), ((), ()))


def _params(*sem):
    return pltpu.CompilerParams(dimension_semantics=sem, vmem_limit_bytes=VMEM_LIMIT)


def _rmsnorm_kernel(x_ref, w_ref, o_ref):
    x = x_ref[...]
    ms = jnp.mean(x * x, axis=-1, keepdims=True)
    o_ref[...] = (x * lax.rsqrt(ms + EPS) * w_ref[...]).astype(o_ref.dtype)


def _rmsnorm(x, w, out_dtype, tm=512):
    m, d = x.shape
    return pl.pallas_call(
        _rmsnorm_kernel,
        grid=(m // tm,),
        in_specs=[pl.BlockSpec((tm, d), lambda i: (i, 0)),
                  pl.BlockSpec((1, d), lambda i: (0, 0))],
        out_specs=pl.BlockSpec((tm, d), lambda i: (i, 0)),
        out_shape=jax.ShapeDtypeStruct((m, d), out_dtype),
        compiler_params=_params("arbitrary"),
        name="rmsnorm",
    )(x, w.reshape(1, d))


def _mm_nt_kernel(x_ref, w_ref, wx_ref, o_ref, wb_ref, *, shift):
    @pl.when(pl.program_id(1) == 0)
    def _():
        tn = wb_ref.shape[0]
        wb_ref[0:tn - shift, :] = w_ref[shift:tn, :].astype(BF16)
        if shift:
            wb_ref[tn - shift:tn, :] = wx_ref[...].astype(BF16)

    o_ref[...] = lax.dot_general(x_ref[...], wb_ref[...], _NT, preferred_element_type=F32).astype(o_ref.dtype)


def _matmul_nt(x, wt, layer, row0, n_rows, out_dtype, tm=1024, tn=1024, shift_rows=32):
    m, k = x.shape
    shift = row0 % tn
    assert shift in (0, shift_rows) and tn % shift_rows == 0
    base = row0 // tn
    return pl.pallas_call(
        functools.partial(_mm_nt_kernel, shift=shift),
        grid=(n_rows // tn, m // tm),
        in_specs=[pl.BlockSpec((tm, k), lambda j, i: (i, 0)),
                  pl.BlockSpec((None, tn, k), lambda j, i: (layer, base + j, 0)),
                  pl.BlockSpec((None, shift_rows, k), lambda j, i: (layer, (base + j + 1) * (tn // shift_rows), 0))],
        out_specs=pl.BlockSpec((tm, tn), lambda j, i: (i, j)),
        out_shape=jax.ShapeDtypeStruct((m, n_rows), out_dtype),
        scratch_shapes=[pltpu.VMEM((tn, k), BF16)],
        compiler_params=_params("arbitrary", "arbitrary"),
        name="in_proj",
    )(x, wt, wt)


def _cast_kernel(x_ref, o_ref):
    o_ref[...] = x_ref[...].astype(o_ref.dtype)


def _cast_bf16(w, layer, rows=512):
    _, k, n = w.shape
    return pl.pallas_call(
        _cast_kernel,
        grid=(k // rows,),
        in_specs=[pl.BlockSpec((None, rows, n), lambda i: (layer, i, 0))],
        out_specs=pl.BlockSpec((rows, n), lambda i: (i, 0)),
        out_shape=jax.ShapeDtypeStruct((k, n), BF16),
        compiler_params=_params("arbitrary"),
        name="cast_bf16",
    )(w)


def _out_proj_kernel(*refs, k_sizes, emit_h):
    nx = len(k_sizes)
    x_refs = refs[:nx]
    w_ref, h_ref, nw_ref = refs[nx:nx + 3]
    out_refs = refs[nx + 3:]
    acc = h_ref[...]
    off = 0
    for x_ref, ks in zip(x_refs, k_sizes):
        acc = acc + jnp.dot(x_ref[...], w_ref[off:off + ks, :], preferred_element_type=F32)
        off += ks
    if emit_h:
        out_refs[0][...] = acc
    hn_ref = out_refs[-1]
    ms = jnp.mean(acc * acc, axis=-1, keepdims=True)
    hn_ref[...] = (acc * lax.rsqrt(ms + EPS) * nw_ref[...]).astype(hn_ref.dtype)


def _out_proj_residual_norm(xs, w_bf16, h, norm_w, last, tm=512):
    m, n = h.shape
    k_sizes = tuple(x.shape[1] for x in xs)
    k = sum(k_sizes)
    row = pl.BlockSpec((tm, n), lambda i: (i, 0))
    if last:
        out_shape = [jax.ShapeDtypeStruct((m, n), F32)]
    else:
        out_shape = [jax.ShapeDtypeStruct((m, n), F32), jax.ShapeDtypeStruct((m, n), BF16)]
    outs = pl.pallas_call(
        functools.partial(_out_proj_kernel, k_sizes=k_sizes, emit_h=not last),
        grid=(m // tm,),
        in_specs=[pl.BlockSpec((tm, ks), lambda i: (i, 0)) for ks in k_sizes]
        + [pl.BlockSpec((k, n), lambda i: (0, 0), pipeline_mode=pl.Buffered(1)), row,
           pl.BlockSpec((1, n), lambda i: (0, 0))],
        out_specs=[row] * len(out_shape),
        out_shape=out_shape,
        compiler_params=_params("arbitrary"),
        name="out_proj",
    )(*xs, w_bf16, h, norm_w.reshape(1, n))
    return outs


def _softplus(x):
    return jnp.maximum(x, 0.0) + jnp.log1p(jnp.exp(-jnp.abs(x)))


def _gates_kernel(x_ref, w_ref, alog_ref, dtb_ref, alog_c_ref, dtb_c_ref, col_ref, row_ref, ws_ref, *, tm):
    nh = DN_V_HEADS
    c = DN_CHUNK

    @pl.when(pl.program_id(0) == 0)
    def _():
        ws_ref[...] = w_ref[...].astype(BF16)

    x = x_ref[...]
    ri = lax.broadcasted_iota(jnp.int32, (c, c), 0)
    ci = lax.broadcasted_iota(jnp.int32, (c, c), 1)
    lower = jnp.where(ci <= ri, 1.0, 0.0).astype(BF16)
    upper = jnp.where(ri <= ci, 1.0, 0.0).astype(BF16)

    def split3(v):
        hi = v.astype(BF16)
        r1 = v - hi.astype(F32)
        mid = r1.astype(BF16)
        lo = (r1 - mid.astype(F32)).astype(BF16)
        return hi, mid, lo

    p = lax.dot_general(x, ws_ref[...], _NT, preferred_element_type=F32)
    beta = jax.nn.sigmoid(p[:, :nh])
    g = -jnp.exp(alog_ref[...]) * _softplus(p[:, nh:2 * nh] + dtb_ref[...])
    pt = lax.dot_general(ws_ref[...], x, _NT, preferred_element_type=F32)
    gt = -jnp.exp(alog_c_ref[...]) * _softplus(pt[nh:2 * nh, :] + dtb_c_ref[...])
    g_parts = split3(g)
    gt_parts = split3(gt)
    for ch in range(tm // c):
        rows = slice(ch * c, (ch + 1) * c)
        gc = sum(jnp.dot(lower, part[rows, :], preferred_element_type=F32) for part in g_parts)
        gct = sum(jnp.dot(part[:, rows], upper, preferred_element_type=F32) for part in gt_parts)
        for hq in range(DN_QK_HEADS):
            col_ref[hq, rows, 0:2] = beta[rows, 2 * hq:2 * hq + 2]
            col_ref[hq, rows, 2:4] = gc[:, 2 * hq:2 * hq + 2]
        row_ref[ch] = gct


def _gates(hn, wt, layer, a_log, dt_bias, batch, seq, tm=512):
    m, k = hn.shape
    nh = DN_V_HEADS
    nch = seq // DN_CHUNK
    ng = 2 * nh
    assert GATE_OFF % ng == 0
    steps = seq // tm
    col, row = pl.pallas_call(
        functools.partial(_gates_kernel, tm=tm),
        grid=(m // tm,),
        in_specs=[pl.BlockSpec((tm, k), lambda i: (i, 0)),
                  pl.BlockSpec((None, ng, k), lambda i: (layer, GATE_OFF // ng, 0)),
                  pl.BlockSpec((1, nh), lambda i: (0, 0)),
                  pl.BlockSpec((1, nh), lambda i: (0, 0)),
                  pl.BlockSpec((nh, 1), lambda i: (0, 0)),
                  pl.BlockSpec((nh, 1), lambda i: (0, 0))],
        out_specs=[pl.BlockSpec((None, DN_QK_HEADS, tm, 4), lambda i: (i // steps, 0, i % steps, 0)),
                   pl.BlockSpec((None, tm // DN_CHUNK, nh, DN_CHUNK), lambda i: (i // steps, i % steps, 0, 0))],
        out_shape=[jax.ShapeDtypeStruct((batch, DN_QK_HEADS, seq, 4), F32),
                   jax.ShapeDtypeStruct((batch, nch, nh, DN_CHUNK), F32)],
        scratch_shapes=[pltpu.VMEM((ng, k), BF16)],
        compiler_params=_params("arbitrary"),
        name="dn_gates",
    )(hn, wt, a_log.reshape(1, nh), dt_bias.reshape(1, nh), a_log.reshape(nh, 1), dt_bias.reshape(nh, 1))
    return col, row


def _conv_silu(pad_ref, col0, ncols, cw_ref, t0, width):
    c = DN_CHUNK
    acc = None
    for i in range(width):
        parts = [pad_ref[col0 // HEAD_DIM + p, pl.ds(t0 + (8 - (width - 1) + i), c, stride=1), :]
                 for p in range(ncols // HEAD_DIM)]
        x = parts[0] if len(parts) == 1 else jnp.concatenate(parts, axis=1)
        term = cw_ref[i:i + 1, :] * x
        acc = term if acc is None else acc + term
    return _silu(acc)


def _l2norm(x):
    return x * lax.rsqrt(jnp.sum(x * x, axis=-1, keepdims=True) + EPS)


def _silu(x):
    return x / (1.0 + jnp.exp(-x))


def _unit_lower_inverse(mats):
    c = mats[0].shape[0]
    ri = lax.broadcasted_iota(jnp.int32, (c, c), 0)
    ci = lax.broadcasted_iota(jnp.int32, (c, c), 1)
    x = ri ^ ci
    zero = jnp.zeros((), BF16)
    eye = jnp.where(ri == ci, 1.0, 0.0).astype(BF16)
    ts = [eye - jnp.where(x < 2, a, zero) for a in mats]
    b = 2
    while b < c:
        level = (x >= b) & (x < 2 * b)
        ls = [jnp.where(level, a, zero) for a in mats]
        xs = [jnp.dot(t, l, preferred_element_type=F32).astype(BF16) for t, l in zip(ts, ls)]
        yield
        ts = [t - jnp.dot(xm, t, preferred_element_type=F32).astype(BF16) for t, xm in zip(ts, xs)]
        yield
        b *= 2
    return ts


def _deltanet_kernel(q_ref, k_ref, v_ref, z_ref, cwq_ref, cwk_ref, cwv_ref, gcol_ref, grow_ref, nw_ref,
                     o_ref, state_ref, gq_ref, cc_ref, op_ref, pad_ref, *, seq, unroll, scan_every):
    c = DN_CHUNK
    dk = HEAD_DIM
    nch = seq // c
    hq = pl.program_id(1)
    ri = lax.broadcasted_iota(jnp.int32, (c, c), 0)
    ci = lax.broadcasted_iota(jnp.int32, (c, c), 1)
    causal = ri >= ci
    strict = ri > ci

    pad_ref[:, 0:8, :] = jnp.zeros((4, 8, dk), F32)

    def pad_body(i, carry):
        r0 = pl.multiple_of(i * c, c)
        pad_ref[0, pl.ds(8 + r0, c), :] = q_ref[pl.ds(r0, c), :].astype(F32)
        pad_ref[1, pl.ds(8 + r0, c), :] = k_ref[pl.ds(r0, c), :].astype(F32)
        pad_ref[2, pl.ds(8 + r0, c), :] = v_ref[pl.ds(r0, c), 0:dk].astype(F32)
        pad_ref[3, pl.ds(8 + r0, c), :] = v_ref[pl.ds(r0, c), dk:2 * dk].astype(F32)
        return carry

    lax.fori_loop(0, nch, pad_body, 0)

    def prep_stages(i):
        chunks = [i * unroll + uu for uu in range(unroll)]
        t0s = [pl.multiple_of(n * c, c) for n in chunks]
        qs = [_l2norm(_conv_silu(pad_ref, 0, dk, cwq_ref, t0, DN_CONV)) * (dk ** -0.5) for t0 in t0s]
        ks = [_l2norm(_conv_silu(pad_ref, dk, dk, cwk_ref, t0, DN_CONV)) for t0 in t0s]
        v2s = [_conv_silu(pad_ref, 2 * dk, 2 * dk, cwv_ref, t0, DN_CONV) for t0 in t0s]
        kbs = [k.astype(BF16) for k in ks]
        kks = [lax.dot_general(kb, kb, _NT, preferred_element_type=F32) for kb in kbs]
        qks = [lax.dot_general(q.astype(BF16), kb, _NT, preferred_element_type=F32) for q, kb in zip(qs, kbs)]
        yield
        gcols = [gcol_ref[pl.ds(t0, c), :] for t0 in t0s]
        chains = [(u, j) for u in range(unroll) for j in range(2)]
        betas = [gcols[u][:, j:j + 1] for u, j in chains]
        gcs = [gcols[u][:, 2 + j:3 + j] for u, j in chains]
        grs = [grow_ref[chunks[u], pl.ds(2 * hq + j, 1), :] for u, j in chains]
        decays = [jnp.exp(jnp.where(causal, gc - gr, NEG_BIG)) for gc, gr in zip(gcs, grs)]
        mats = [jnp.where(strict, beta * kks[u] * decay, 0.0).astype(BF16)
                for (u, j), beta, decay in zip(chains, betas, decays)]
        ts = yield from _unit_lower_inverse(mats)
        egs = [jnp.exp(gc) for gc in gcs]
        rhss = [jnp.concatenate([v2s[u][:, j * dk:(j + 1) * dk] * beta, ks[u] * (beta * eg)], axis=1)
                for (u, j), beta, eg in zip(chains, betas, egs)]
        uws = [jnp.dot(t, rhs.astype(BF16), preferred_element_type=F32).astype(BF16)
               for t, rhs in zip(ts, rhss)]
        yield
        kds = [(ks[u] * jnp.exp(gc[c - 1:c, :] - gc)).astype(BF16) for (u, j), gc in zip(chains, gcs)]
        cgs = [lax.dot_general(kd, uw, _TN, preferred_element_type=F32) for kd, uw in zip(kds, uws)]
        yield
        oxs = [jnp.dot((qks[u] * decay).astype(BF16), uw, preferred_element_type=F32)
               for (u, j), decay, uw in zip(chains, decays, uws)]
        yield
        for (u, j), cg, ox, eg in zip(chains, cgs, oxs, egs):
            n = chunks[u]
            gq_ref[n, j, 0:c, :] = cg[:, dk:].astype(BF16)
            gq_ref[n, j, c:2 * c, :] = (qs[u] * eg - ox[:, dk:]).astype(BF16)
            cc_ref[n, j] = cg[:, :dk]
            op_ref[n, j] = ox[:, :dk]

    def scan_steps(i):
        for uu in range(unroll):
            n = i * unroll + uu
            t0 = pl.multiple_of(n * c, c)
            z2 = z_ref[pl.ds(t0, c), :].astype(F32)
            for j in range(2):
                s = state_ref[j]
                prod = jnp.dot(gq_ref[n, j], s.astype(BF16), preferred_element_type=F32)
                gr = grow_ref[n, pl.ds(2 * hq + j, 1), :]
                state_ref[j] = s * jnp.exp(gr[:, c - 1:c]) - prod[:c] + cc_ref[n, j]
                o = prod[c:] + op_ref[n, j]
                o = o * lax.rsqrt(jnp.mean(o * o, axis=-1, keepdims=True) + EPS) * nw_ref[...]
                z = z2[:, j * dk:(j + 1) * dk]
                o_ref[pl.ds(t0, c), j * dk:(j + 1) * dk] = (o * _silu(z)).astype(o_ref.dtype)
            yield

    def run_interleaved(prep, scan, every):
        for count, _ in enumerate(prep, start=1):
            if count % every == 0:
                next(scan, None)
        for _ in scan:
            pass

    state_ref[...] = jnp.zeros_like(state_ref)
    groups = nch // unroll
    for _ in prep_stages(0):
        pass

    def pipelined_body(i, carry):
        run_interleaved(prep_stages(i), scan_steps(i - 1), every=scan_every)
        return carry

    lax.fori_loop(1, groups, pipelined_body, 0)
    for _ in scan_steps(groups - 1):
        pass


def _deltanet(proj_a, conv_w_t, gcol, grow, norm_w, batch, seq, unroll=8, scan_every=1):
    dk = HEAD_DIM
    nq = DN_QK_HEADS
    c = DN_CHUNK
    nch = seq // c
    return pl.pallas_call(
        functools.partial(_deltanet_kernel, seq=seq, unroll=unroll, scan_every=scan_every),
        grid=(batch, nq),
        in_specs=[pl.BlockSpec((seq, dk), lambda b, h: (b, h)),
                  pl.BlockSpec((seq, dk), lambda b, h: (b, nq + h)),
                  pl.BlockSpec((seq, 2 * dk), lambda b, h: (b, nq + h)),
                  pl.BlockSpec((seq, 2 * dk), lambda b, h: (b, 2 * nq + h)),
                  pl.BlockSpec((DN_CONV, dk), lambda b, h: (0, h)),
                  pl.BlockSpec((DN_CONV, dk), lambda b, h: (0, nq + h)),
                  pl.BlockSpec((DN_CONV, 2 * dk), lambda b, h: (0, nq + h)),
                  pl.BlockSpec((None, None, seq, 4), lambda b, h: (b, h, 0, 0)),
                  pl.BlockSpec((None, nch, DN_V_HEADS, c), lambda b, h: (b, 0, 0, 0)),
                  pl.BlockSpec((1, dk), lambda b, h: (0, 0))],
        out_specs=pl.BlockSpec((seq, 2 * dk), lambda b, h: (b, h)),
        out_shape=jax.ShapeDtypeStruct((batch * seq, A_V), BF16),
        scratch_shapes=[pltpu.VMEM((2, dk, dk), F32),
                        pltpu.VMEM((nch, 2, 2 * c, dk), BF16),
                        pltpu.VMEM((nch, 2, dk, dk), F32),
                        pltpu.VMEM((nch, 2, c, dk), F32),
                        pltpu.VMEM((4, 8 + seq, dk), F32)],
        compiler_params=_params("arbitrary", "arbitrary"),
        name="deltanet",
    )(proj_a, proj_a, proj_a, proj_a, conv_w_t, conv_w_t, conv_w_t, gcol, grow, norm_w.reshape(1, dk))


def _rope_table_kernel(cos_ref, sin_ref):
    shape = cos_ref.shape
    pos = lax.broadcasted_iota(jnp.int32, shape, 0).astype(F32)
    lane = lax.broadcasted_iota(jnp.int32, shape, 1)
    idx = (lane % ROPE_HALF).astype(F32)
    inv = jnp.exp(idx * (-2.0 / ROPE_DIM) * jnp.log(jnp.float32(ROPE_THETA)))
    ang = pos * inv
    cos = jnp.cos(ang)
    sin = jnp.sin(ang)
    cos_ref[...] = jnp.where(lane < ROPE_DIM, cos, 1.0)
    sin_ref[...] = jnp.where(lane < ROPE_HALF, -sin, jnp.where(lane < ROPE_DIM, sin, 0.0))


def _rope_tables(seq):
    shp = jax.ShapeDtypeStruct((seq, HEAD_DIM), F32)
    return pl.pallas_call(_rope_table_kernel, out_shape=[shp, shp], name="rope_tables")()


def _attn_kernel(q0_ref, q1_ref, q2_ref, k_ref, v_ref, z_ref, cos_ref, sin_ref, o_ref,
                 qs_ref, ks_ref, vs_ref, m_ref, l_ref, acc_ref, bias_ref, *, seq, unroll):
    wb = SW_BLOCK
    dh = HEAD_DIM
    rows = 256
    pr = lax.broadcasted_iota(jnp.int32, (dh, dh), 0)
    pc = lax.broadcasted_iota(jnp.int32, (dh, dh), 1)
    swap = jnp.where(((pc < ROPE_HALF) & (pr == pc + ROPE_HALF))
                     | ((pc >= ROPE_HALF) & (pc < ROPE_DIM) & (pr == pc - ROPE_HALF)), 1.0, 0.0).astype(BF16)

    def rope(x, cos, sin):
        swapped = jnp.dot(x.astype(BF16), swap, preferred_element_type=F32)
        return x * cos + swapped * sin

    def rope_body(i, carry):
        r0 = pl.multiple_of(i * rows, rows)
        cos = cos_ref[pl.ds(r0, rows), :]
        sin = sin_ref[pl.ds(r0, rows), :]
        for g, q_ref in enumerate((q0_ref, q1_ref, q2_ref)):
            qs_ref[g, pl.ds(r0, rows), :] = rope(q_ref[pl.ds(r0, rows), :].astype(F32), cos, sin) * (dh ** -0.5)
        ks_ref[pl.ds(r0, rows), :] = rope(k_ref[pl.ds(r0, rows), :].astype(F32), cos, sin)
        vs_ref[pl.ds(r0, rows), :] = v_ref[pl.ds(r0, rows), :].astype(F32)
        return carry

    lax.fori_loop(0, seq // rows, rope_body, 0)

    delta = lax.broadcasted_iota(jnp.int32, (wb, 2 * wb), 1) - lax.broadcasted_iota(jnp.int32, (wb, 2 * wb), 0)
    bias_ref[0] = jnp.where((delta >= 0) & (delta <= wb), 0.0, NEG_BIG)
    bias_ref[1] = jnp.where(delta <= 0, 0.0, NEG_BIG)

    order = sorted(range(len(SW_DILATIONS)), key=lambda gi: -SW_DILATIONS[gi])
    for pos_in_order, g in enumerate(order):
        dil = SW_DILATIONS[g]
        first = pos_in_order == 0
        last = pos_in_order == len(order) - 1
        nb = max(seq // dil // wb, 1)
        has_prev = nb > 1
        kw = 2 * wb if has_prev else wb
        ones = jnp.ones((kw, dh), BF16)

        def block_body(it, carry, g=g, dil=dil, nb=nb, has_prev=has_prev, kw=kw, ones=ones,
                       first=first, last=last):
            def rows_of(ref, s, count):
                if dil == 1:
                    return ref[pl.ds(s, count), :]
                return ref[pl.ds(s, count, stride=dil), :]

            idxs = [it * unroll + u for u in range(unroll)]
            ns = [idx & (nb - 1) for idx in idxs]
            starts = [lax.shift_right_logical(idx, nb.bit_length() - 1) + n * (dil * wb) for idx, n in zip(idxs, ns)]
            if dil == 1:
                starts = [pl.multiple_of(s, wb) for s in starts]
            if has_prev:
                kstarts = [jnp.where(n > 0, s - dil * wb, s) for s, n in zip(starts, ns)]
                biases = [bias_ref[jnp.where(n > 0, 0, 1)] for n in ns]
            else:
                kstarts = starts
                biases = [bias_ref[1, :, 0:wb] for _ in ns]
            qs = [rows_of(qs_ref.at[g], s, wb).astype(BF16) for s in starts]
            kwins = [rows_of(ks_ref, s, kw).astype(BF16) for s in kstarts]
            vaugs = [jnp.concatenate([rows_of(vs_ref, s, kw).astype(BF16), ones], axis=1) for s in kstarts]
            scs = [lax.dot_general(q, kwin, _NT, preferred_element_type=F32) + bias
                   for bias, q, kwin in zip(biases, qs, kwins)]
            m_blks = [jnp.max(sc, axis=-1, keepdims=True) for sc in scs]
            ps = [jnp.exp(sc - m).astype(BF16) for sc, m in zip(scs, m_blks)]
            pvs = [jnp.dot(p, vaug, preferred_element_type=F32) for p, vaug in zip(ps, vaugs)]
            for s, m_blk, pv in zip(starts, m_blks, pvs):
                m_blk = jnp.broadcast_to(m_blk, (wb, dh))
                acc_blk = pv[:, :dh]
                l_blk = pv[:, dh:]
                sl = pl.ds(s, wb) if dil == 1 else pl.ds(s, wb, stride=dil)
                if first:
                    m_ref[sl, :] = m_blk
                    l_ref[sl, :] = l_blk
                    acc_ref[sl, :] = acc_blk
                    continue
                m_old = m_ref[sl, :]
                m_new = jnp.maximum(m_old, m_blk)
                a_old = jnp.exp(m_old - m_new)
                a_blk = jnp.exp(m_blk - m_new)
                l_new = a_old * l_ref[sl, :] + a_blk * l_blk
                acc_new = a_old * acc_ref[sl, :] + a_blk * acc_blk
                if last:
                    z = z_ref[sl, :].astype(F32)
                    o_ref[sl, :] = (acc_new / l_new * _silu(z)).astype(o_ref.dtype)
                else:
                    m_ref[sl, :] = m_new
                    l_ref[sl, :] = l_new
                    acc_ref[sl, :] = acc_new
            return carry

        lax.fori_loop(0, dil * nb // unroll, block_body, 0)


def _dilated_attention(proj_b, cos_tab, sin_tab, batch, seq, unroll=8):
    dh = HEAD_DIM
    nh = SW_HEADS
    ng = len(SW_DILATIONS)

    def col(block):
        return pl.BlockSpec((seq, dh), lambda b, h: (b, block * nh + h))

    tab = pl.BlockSpec((seq, dh), lambda b, h: (0, 0))
    return pl.pallas_call(
        functools.partial(_attn_kernel, seq=seq, unroll=unroll),
        grid=(batch, nh),
        in_specs=[col(0), col(1), col(2), col(ng), col(ng + 1), col(ng + 2), tab, tab],
        out_specs=pl.BlockSpec((seq, dh), lambda b, h: (b, h)),
        out_shape=jax.ShapeDtypeStruct((batch * seq, nh * dh), BF16),
        scratch_shapes=[pltpu.VMEM((ng, seq, dh), F32), pltpu.VMEM((seq, dh), F32), pltpu.VMEM((seq, dh), F32),
                        pltpu.VMEM((seq, dh), F32), pltpu.VMEM((seq, dh), F32), pltpu.VMEM((seq, dh), F32),
                        pltpu.VMEM((2, SW_BLOCK, 2 * SW_BLOCK), F32)],
        compiler_params=_params("arbitrary", "arbitrary"),
        name="dilated_attn",
    )(proj_b, proj_b, proj_b, proj_b, proj_b, proj_b, cos_tab, sin_tab)


def _shortconv_kernel(x_ref, wb_ref, wc_ref, wu_ref, wz_ref, cw_ref, o_ref, wq_ref, halo_ref, *, steps_per_seq):
    i = pl.program_id(1)

    @pl.when(i == 0)
    def _():
        for g, w_ref in enumerate((wb_ref, wc_ref, wu_ref, wz_ref)):
            wq_ref[g] = w_ref[...].astype(BF16)

    x = x_ref[...]
    tm = x.shape[0]
    gate_c = jnp.dot(x, wq_ref[1], preferred_element_type=F32)
    u = jnp.dot(x, wq_ref[2], preferred_element_type=F32)
    cu = gate_c * u
    prev = jnp.where(i % steps_per_seq != 0, halo_ref[...], 0.0)
    halo_ref[...] = cu[tm - 8:, :]
    xx = jnp.concatenate([prev, cu], axis=0)
    conv = cw_ref[SC_CONV - 1:SC_CONV, :] * cu
    for t in range(SC_CONV - 1):
        lo = 8 - (SC_CONV - 1) + t
        conv = conv + cw_ref[t:t + 1, :] * xx[lo:lo + tm, :]
    gate_b = jnp.dot(x, wq_ref[0], preferred_element_type=F32)
    z = jnp.dot(x, wq_ref[3], preferred_element_type=F32)
    o_ref[...] = (gate_b * conv * _silu(z)).astype(o_ref.dtype)


def _shortconv_mix(hn, w_in, layer, conv_w_t, seq, tm=1024, tn=256):
    m, k = hn.shape
    nblk = SC_WIDTH // tn

    def wspec(g):
        return pl.BlockSpec((None, k, tn), lambda j, i: (layer, 0, g * nblk + j))

    return pl.pallas_call(
        functools.partial(_shortconv_kernel, steps_per_seq=seq // tm),
        grid=(nblk, m // tm),
        in_specs=[pl.BlockSpec((tm, k), lambda j, i: (i, 0)), wspec(0), wspec(1), wspec(2), wspec(3),
                  pl.BlockSpec((None, SC_CONV, tn), lambda j, i: (layer, 0, j))],
        out_specs=pl.BlockSpec((tm, tn), lambda j, i: (i, j)),
        out_shape=jax.ShapeDtypeStruct((m, SC_WIDTH), BF16),
        scratch_shapes=[pltpu.VMEM((4, k, tn), BF16), pltpu.VMEM((8, tn), F32)],
        compiler_params=_params("arbitrary", "arbitrary"),
        name="shortconv_mix",
    )(hn, w_in, w_in, w_in, w_in, conv_w_t)


def kernel(x, norm_w, hyb_w_in, dn_conv_w, dn_a_log, dn_dt_bias, dn_norm_w, hyb_w_out, sc_w_in, sc_conv_w,
           sc_w_out, final_norm_w):
    batch, seq, d = x.shape
    h = x.reshape(batch * seq, d)
    cos_tab, sin_tab = _rope_tables(seq)
    dn_conv_t = jnp.swapaxes(dn_conv_w, 1, 2)
    sc_conv_t = jnp.swapaxes(sc_conv_w, 1, 2)
    hyb_wt = jnp.swapaxes(hyb_w_in, 1, 2)
    hn = _rmsnorm(h, norm_w[0], BF16)
    for layer in range(DEPTH):
        li = layer // 2
        last = layer == DEPTH - 1
        next_norm = final_norm_w if last else norm_w[layer + 1]
        if layer % 2 == 0:
            proj_a = _matmul_nt(hn, hyb_wt, li, 0, A_WIDTH, BF16)
            proj_b = _matmul_nt(hn, hyb_wt, li, B_OFF, B_WIDTH, BF16)
            gcol, grow = _gates(hn, hyb_wt, li, dn_a_log[li], dn_dt_bias[li], batch, seq)
            y_a = _deltanet(proj_a, dn_conv_t[li], gcol, grow, dn_norm_w[li], batch, seq)
            y_b = _dilated_attention(proj_b, cos_tab, sin_tab, batch, seq)
            outs = _out_proj_residual_norm([y_a, y_b], _cast_bf16(hyb_w_out, li), h, next_norm, last)
        else:
            y = _shortconv_mix(hn, sc_w_in, li, sc_conv_t, seq)
            outs = _out_proj_residual_norm([y], _cast_bf16(sc_w_out, li), h, next_norm, last)
        if last:
            return outs[0].reshape(batch, seq, d)
        h, hn = outs
```

```python
import functools

import jax
import jax.numpy as jnp
from jax import lax
from jax.experimental import pallas as pl
from jax.experimental.pallas import tpu as pltpu

F32 = jnp.float32
BF16 = jnp.bfloat16

D_MODEL = 2048
DEPTH = 4
HEAD_DIM = 128
EPS = 1e-6
DN_QK_HEADS = 8
DN_V_HEADS = 16
DN_CONV = 4
DN_CHUNK = 128
SW_HEADS = 8
SW_DILATIONS = (1, 4, 16)
SW_BLOCK = 128
ROPE_THETA = 500000.0
ROPE_DIM = HEAD_DIM // 4
ROPE_HALF = ROPE_DIM // 2
SC_WIDTH = 3072
SC_CONV = 3

A_Q = DN_QK_HEADS * HEAD_DIM
A_V = DN_V_HEADS * HEAD_DIM
A_WIDTH = 2 * A_Q + 2 * A_V
GATE_OFF = A_WIDTH
B_OFF = A_WIDTH + 2 * DN_V_HEADS
B_WIDTH = (len(SW_DILATIONS) + 3) * SW_HEADS * HEAD_DIM
HYB_MIX = A_V + SW_HEADS * HEAD_DIM

LANES = 128
VMEM_LIMIT = 56 * 1024 * 1024
NEG_BIG = -1e30

_NT = (((1,), (1,)), ((), ()))
_TN = (((0,), (0,)), ((), ()))


def _params(*sem):
    return pltpu.CompilerParams(dimension_semantics=sem, vmem_limit_bytes=VMEM_LIMIT)


def _rmsnorm_kernel(x_ref, w_ref, o_ref):
    x = x_ref[...]
    ms = jnp.mean(x * x, axis=-1, keepdims=True)
    o_ref[...] = (x * lax.rsqrt(ms + EPS) * w_ref[...]).astype(o_ref.dtype)


def _rmsnorm(x, w, out_dtype, tm=512):
    m, d = x.shape
    return pl.pallas_call(
        _rmsnorm_kernel,
        grid=(m // tm,),
        in_specs=[pl.BlockSpec((tm, d), lambda i: (i, 0)),
                  pl.BlockSpec((1, d), lambda i: (0, 0))],
        out_specs=pl.BlockSpec((tm, d), lambda i: (i, 0)),
        out_shape=jax.ShapeDtypeStruct((m, d), out_dtype),
        compiler_params=_params("arbitrary"),
        name="rmsnorm",
    )(x, w.reshape(1, d))


def _mm_nt_kernel(x_ref, w_ref, wx_ref, *rest, shift, with_cast):
    if with_cast:
        cast_src_ref, o_ref, cast_out_ref, wb_ref = rest
        cast_out_ref[...] = cast_src_ref[...].astype(BF16)
    else:
        o_ref, wb_ref = rest

    @pl.when(pl.program_id(1) == 0)
    def _():
        tn = wb_ref.shape[0]
        wb_ref[0:tn - shift, :] = w_ref[shift:tn, :].astype(BF16)
        if shift:
            wb_ref[tn - shift:tn, :] = wx_ref[...].astype(BF16)

    o_ref[...] = lax.dot_general(x_ref[...], wb_ref[...], _NT, preferred_element_type=F32).astype(o_ref.dtype)


def _matmul_nt(x, wt, layer, row0, n_rows, out_dtype, cast_src=None, tm=1024, tn=1024, shift_rows=32):
    m, k = x.shape
    shift = row0 % tn
    assert shift in (0, shift_rows) and tn % shift_rows == 0
    base = row0 // tn
    n_j, n_i = n_rows // tn, m // tm
    in_specs = [pl.BlockSpec((tm, k), lambda j, i: (i, 0)),
                pl.BlockSpec((None, tn, k), lambda j, i: (layer, base + j, 0)),
                pl.BlockSpec((None, shift_rows, k), lambda j, i: (layer, (base + j + 1) * (tn // shift_rows), 0))]
    out_specs = [pl.BlockSpec((tm, tn), lambda j, i: (i, j))]
    out_shape = [jax.ShapeDtypeStruct((m, n_rows), out_dtype)]
    args = [x, wt, wt]
    if cast_src is not None:
        _, r, c = cast_src.shape
        rows = r // (n_j * n_i)
        assert rows * n_j * n_i == r
        in_specs.append(pl.BlockSpec((None, rows, c), lambda j, i: (layer, j * n_i + i, 0)))
        out_specs.append(pl.BlockSpec((rows, c), lambda j, i: (j * n_i + i, 0)))
        out_shape.append(jax.ShapeDtypeStruct((r, c), BF16))
        args.append(cast_src)
    outs = pl.pallas_call(
        functools.partial(_mm_nt_kernel, shift=shift, with_cast=cast_src is not None),
        grid=(n_j, n_i),
        in_specs=in_specs,
        out_specs=out_specs,
        out_shape=out_shape,
        scratch_shapes=[pltpu.VMEM((tn, k), BF16)],
        compiler_params=_params("arbitrary", "arbitrary"),
        name="in_proj",
    )(*args)
    return outs if cast_src is not None else outs[0]


def _out_proj_kernel(*refs, k_sizes, emit_h):
    nx = len(k_sizes)
    x_refs = refs[:nx]
    w_ref, h_ref, nw_ref = refs[nx:nx + 3]
    out_refs = refs[nx + 3:]
    acc = h_ref[...]
    off = 0
    for x_ref, ks in zip(x_refs, k_sizes):
        acc = acc + jnp.dot(x_ref[...], w_ref[off:off + ks, :], preferred_element_type=F32)
        off += ks
    if emit_h:
        out_refs[0][...] = acc
    hn_ref = out_refs[-1]
    ms = jnp.mean(acc * acc, axis=-1, keepdims=True)
    hn_ref[...] = (acc * lax.rsqrt(ms + EPS) * nw_ref[...]).astype(hn_ref.dtype)


def _out_proj_residual_norm(xs, w_bf16, h, norm_w, last, tm=512):
    m, n = h.shape
    k_sizes = tuple(x.shape[1] for x in xs)
    k = sum(k_sizes)
    row = pl.BlockSpec((tm, n), lambda i: (i, 0))
    if last:
        out_shape = [jax.ShapeDtypeStruct((m, n), F32)]
    else:
        out_shape = [jax.ShapeDtypeStruct((m, n), F32), jax.ShapeDtypeStruct((m, n), BF16)]
    outs = pl.pallas_call(
        functools.partial(_out_proj_kernel, k_sizes=k_sizes, emit_h=not last),
        grid=(m // tm,),
        in_specs=[pl.BlockSpec((tm, ks), lambda i: (i, 0)) for ks in k_sizes]
        + [pl.BlockSpec((k, n), lambda i: (0, 0), pipeline_mode=pl.Buffered(1)), row,
           pl.BlockSpec((1, n), lambda i: (0, 0))],
        out_specs=[row] * len(out_shape),
        out_shape=out_shape,
        compiler_params=_params("arbitrary"),
        name="out_proj",
    )(*xs, w_bf16, h, norm_w.reshape(1, n))
    return outs


def _softplus(x):
    return jnp.maximum(x, 0.0) + jnp.log1p(jnp.exp(-jnp.abs(x)))


def _gates_kernel(x_ref, w_ref, alog_ref, dtb_ref, col_ref, row_ref, ws_ref, *, tm):
    nh = DN_V_HEADS
    c = DN_CHUNK

    @pl.when(pl.program_id(0) == 0)
    def _():
        ws_ref[...] = w_ref[...].astype(BF16)

    ri = lax.broadcasted_iota(jnp.int32, (c, c), 0)
    ci = lax.broadcasted_iota(jnp.int32, (c, c), 1)
    lower = jnp.where(ci <= ri, 1.0, 0.0).astype(BF16)

    p = lax.dot_general(x_ref[...], ws_ref[...], _NT, preferred_element_type=F32)
    beta = jax.nn.sigmoid(p[:, :nh])
    g = -jnp.exp(alog_ref[...]) * _softplus(p + dtb_ref[...])
    hi = g.astype(BF16)
    r1 = g - hi.astype(F32)
    mid = r1.astype(BF16)
    lo = (r1 - mid.astype(F32)).astype(BF16)
    for ch in range(tm // c):
        rows = slice(ch * c, (ch + 1) * c)
        gc = sum(jnp.dot(lower, part[rows, :], preferred_element_type=F32) for part in (hi, mid, lo))
        for hq in range(DN_QK_HEADS):
            col_ref[hq, rows, 0:2] = beta[rows, 2 * hq:2 * hq + 2]
            col_ref[hq, rows, 2:4] = gc[:, nh + 2 * hq:nh + 2 * hq + 2]
        row_ref[ch] = gc.T[nh:2 * nh, :]


def _gates(hn, wt, layer, a_log, dt_bias, batch, seq, tm=512):
    m, k = hn.shape
    nh = DN_V_HEADS
    nch = seq // DN_CHUNK
    assert GATE_OFF % LANES == 0
    steps = seq // tm

    def at_alpha_lanes(v):
        return jnp.zeros((1, LANES), F32).at[0, nh:2 * nh].set(v)

    col, row = pl.pallas_call(
        functools.partial(_gates_kernel, tm=tm),
        grid=(m // tm,),
        in_specs=[pl.BlockSpec((tm, k), lambda i: (i, 0)),
                  pl.BlockSpec((None, LANES, k), lambda i: (layer, GATE_OFF // LANES, 0)),
                  pl.BlockSpec((1, LANES), lambda i: (0, 0)),
                  pl.BlockSpec((1, LANES), lambda i: (0, 0))],
        out_specs=[pl.BlockSpec((None, DN_QK_HEADS, tm, 4), lambda i: (i // steps, 0, i % steps, 0)),
                   pl.BlockSpec((None, tm // DN_CHUNK, nh, DN_CHUNK), lambda i: (i // steps, i % steps, 0, 0))],
        out_shape=[jax.ShapeDtypeStruct((batch, DN_QK_HEADS, seq, 4), F32),
                   jax.ShapeDtypeStruct((batch, nch, nh, DN_CHUNK), F32)],
        scratch_shapes=[pltpu.VMEM((LANES, k), BF16)],
        compiler_params=_params("arbitrary"),
        name="dn_gates",
    )(hn, wt, at_alpha_lanes(a_log), at_alpha_lanes(dt_bias))
    return col, row


def _conv_silu(pad_ref, col0, ncols, cw_ref, t0, width):
    c = DN_CHUNK
    acc = None
    for i in range(width):
        parts = [pad_ref[col0 // HEAD_DIM + p, pl.ds(t0 + (8 - (width - 1) + i), c, stride=1), :]
                 for p in range(ncols // HEAD_DIM)]
        x = parts[0] if len(parts) == 1 else jnp.concatenate(parts, axis=1)
        term = cw_ref[i:i + 1, :] * x
        acc = term if acc is None else acc + term
    return _silu(acc)


def _l2norm(x):
    return x * lax.rsqrt(jnp.sum(x * x, axis=-1, keepdims=True) + EPS)


def _silu(x):
    return x / (1.0 + jnp.exp(-x))


def _unit_lower_inverse(mats):
    c = mats[0].shape[0]
    ri = lax.broadcasted_iota(jnp.int32, (c, c), 0)
    ci = lax.broadcasted_iota(jnp.int32, (c, c), 1)
    x = ri ^ ci
    zero = jnp.zeros((), BF16)
    eye = jnp.where(ri == ci, 1.0, 0.0).astype(BF16)
    ts = [eye - jnp.where(x < 2, a, zero) for a in mats]
    b = 2
    while b < c:
        level = (x >= b) & (x < 2 * b)
        ls = [jnp.where(level, a, zero) for a in mats]
        xs = [jnp.dot(t, l, preferred_element_type=F32).astype(BF16) for t, l in zip(ts, ls)]
        yield
        ts = [t - jnp.dot(xm, t, preferred_element_type=F32).astype(BF16) for t, xm in zip(ts, xs)]
        yield
        b *= 2
    return ts


def _deltanet_kernel(q_ref, k_ref, v_ref, z_ref, cwq_ref, cwk_ref, cwv_ref, gcol_ref, grow_ref, nw_ref,
                     o_ref, state_ref, gq_ref, cc_ref, op_ref, pad_ref, *, seq, unroll, scan_every):
    c = DN_CHUNK
    dk = HEAD_DIM
    nch = seq // c
    hq = pl.program_id(1)
    ri = lax.broadcasted_iota(jnp.int32, (c, c), 0)
    ci = lax.broadcasted_iota(jnp.int32, (c, c), 1)
    causal = ri >= ci
    strict = ri > ci

    pad_ref[:, 0:8, :] = jnp.zeros((4, 8, dk), F32)

    def pad_body(i, carry):
        r0 = pl.multiple_of(i * c, c)
        pad_ref[0, pl.ds(8 + r0, c), :] = q_ref[pl.ds(r0, c), :].astype(F32)
        pad_ref[1, pl.ds(8 + r0, c), :] = k_ref[pl.ds(r0, c), :].astype(F32)
        pad_ref[2, pl.ds(8 + r0, c), :] = v_ref[pl.ds(r0, c), 0:dk].astype(F32)
        pad_ref[3, pl.ds(8 + r0, c), :] = v_ref[pl.ds(r0, c), dk:2 * dk].astype(F32)
        return carry

    lax.fori_loop(0, nch, pad_body, 0)

    def prep_stages(i):
        chunks = [i * unroll + uu for uu in range(unroll)]
        t0s = [pl.multiple_of(n * c, c) for n in chunks]
        qs = [_l2norm(_conv_silu(pad_ref, 0, dk, cwq_ref, t0, DN_CONV)) * (dk ** -0.5) for t0 in t0s]
        ks = [_l2norm(_conv_silu(pad_ref, dk, dk, cwk_ref, t0, DN_CONV)) for t0 in t0s]
        v2s = [_conv_silu(pad_ref, 2 * dk, 2 * dk, cwv_ref, t0, DN_CONV) for t0 in t0s]
        kbs = [k.astype(BF16) for k in ks]
        kks = [lax.dot_general(kb, kb, _NT, preferred_element_type=F32) for kb in kbs]
        qks = [lax.dot_general(q.astype(BF16), kb, _NT, preferred_element_type=F32) for q, kb in zip(qs, kbs)]
        yield
        gcols = [gcol_ref[pl.ds(t0, c), :] for t0 in t0s]
        chains = [(u, j) for u in range(unroll) for j in range(2)]
        betas = [gcols[u][:, j:j + 1] for u, j in chains]
        gcs = [gcols[u][:, 2 + j:3 + j] for u, j in chains]
        grs = [grow_ref[chunks[u], pl.ds(2 * hq + j, 1), :] for u, j in chains]
        decays = [jnp.exp(jnp.where(causal, gc - gr, NEG_BIG)) for gc, gr in zip(gcs, grs)]
        mats = [jnp.where(strict, beta * kks[u] * decay, 0.0).astype(BF16)
                for (u, j), beta, decay in zip(chains, betas, decays)]
        ts = yield from _unit_lower_inverse(mats)
        egs = [jnp.exp(gc) for gc in gcs]
        rhss = [jnp.concatenate([v2s[u][:, j * dk:(j + 1) * dk] * beta, ks[u] * (beta * eg)], axis=1)
                for (u, j), beta, eg in zip(chains, betas, egs)]
        uws = [jnp.dot(t, rhs.astype(BF16), preferred_element_type=F32).astype(BF16)
               for t, rhs in zip(ts, rhss)]
        yield
        kds = [(ks[u] * jnp.exp(gc[c - 1:c, :] - gc)).astype(BF16) for (u, j), gc in zip(chains, gcs)]
        cgs = [lax.dot_general(kd, uw, _TN, preferred_element_type=F32) for kd, uw in zip(kds, uws)]
        yield
        oxs = [jnp.dot((qks[u] * decay).astype(BF16), uw, preferred_element_type=F32)
               for (u, j), decay, uw in zip(chains, decays, uws)]
        yield
        for (u, j), cg, ox, eg in zip(chains, cgs, oxs, egs):
            n = chunks[u]
            gq_ref[n, j, 0:c, :] = cg[:, dk:].astype(BF16)
            gq_ref[n, j, c:2 * c, :] = (qs[u] * eg - ox[:, dk:]).astype(BF16)
            cc_ref[n, j] = cg[:, :dk]
            op_ref[n, j] = ox[:, :dk]

    def scan_steps(i):
        for uu in range(unroll):
            n = i * unroll + uu
            t0 = pl.multiple_of(n * c, c)
            z2 = z_ref[pl.ds(t0, c), :].astype(F32)
            for j in range(2):
                s = state_ref[j]
                prod = jnp.dot(gq_ref[n, j], s.astype(BF16), preferred_element_type=F32)
                gr = grow_ref[n, pl.ds(2 * hq + j, 1), :]
                state_ref[j] = s * jnp.exp(gr[:, c - 1:c]) - prod[:c] + cc_ref[n, j]
                o = prod[c:] + op_ref[n, j]
                o = o * lax.rsqrt(jnp.mean(o * o, axis=-1, keepdims=True) + EPS) * nw_ref[...]
                z = z2[:, j * dk:(j + 1) * dk]
                o_ref[pl.ds(t0, c), j * dk:(j + 1) * dk] = (o * _silu(z)).astype(o_ref.dtype)
            yield

    def run_interleaved(prep, scan, every):
        for count, _ in enumerate(prep, start=1):
            if count % every == 0:
                next(scan, None)
        for _ in scan:
            pass

    state_ref[...] = jnp.zeros_like(state_ref)
    groups = nch // unroll
    for _ in prep_stages(0):
        pass

    def pipelined_body(i, carry):
        run_interleaved(prep_stages(i), scan_steps(i - 1), every=scan_every)
        return carry

    lax.fori_loop(1, groups, pipelined_body, 0)
    for _ in scan_steps(groups - 1):
        pass


def _deltanet(proj_a, conv_w_t, gcol, grow, norm_w, batch, seq, unroll=8, scan_every=1):
    dk = HEAD_DIM
    nq = DN_QK_HEADS
    c = DN_CHUNK
    nch = seq // c
    return pl.pallas_call(
        functools.partial(_deltanet_kernel, seq=seq, unroll=unroll, scan_every=scan_every),
        grid=(batch, nq),
        in_specs=[pl.BlockSpec((seq, dk), lambda b, h: (b, h)),
                  pl.BlockSpec((seq, dk), lambda b, h: (b, nq + h)),
                  pl.BlockSpec((seq, 2 * dk), lambda b, h: (b, nq + h)),
                  pl.BlockSpec((seq, 2 * dk), lambda b, h: (b, 2 * nq + h)),
                  pl.BlockSpec((DN_CONV, dk), lambda b, h: (0, h)),
                  pl.BlockSpec((DN_CONV, dk), lambda b, h: (0, nq + h)),
                  pl.BlockSpec((DN_CONV, 2 * dk), lambda b, h: (0, nq + h)),
                  pl.BlockSpec((None, None, seq, 4), lambda b, h: (b, h, 0, 0)),
                  pl.BlockSpec((None, nch, DN_V_HEADS, c), lambda b, h: (b, 0, 0, 0)),
                  pl.BlockSpec((1, dk), lambda b, h: (0, 0))],
        out_specs=pl.BlockSpec((seq, 2 * dk), lambda b, h: (b, h)),
        out_shape=jax.ShapeDtypeStruct((batch * seq, A_V), BF16),
        scratch_shapes=[pltpu.VMEM((2, dk, dk), F32),
                        pltpu.VMEM((nch, 2, 2 * c, dk), BF16),
                        pltpu.VMEM((nch, 2, dk, dk), F32),
                        pltpu.VMEM((nch, 2, c, dk), F32),
                        pltpu.VMEM((4, 8 + seq, dk), F32)],
        compiler_params=_params("arbitrary", "arbitrary"),
        name="deltanet",
    )(proj_a, proj_a, proj_a, proj_a, conv_w_t, conv_w_t, conv_w_t, gcol, grow, norm_w.reshape(1, dk))


def _rope_table_kernel(cos_ref, sin_ref):
    shape = cos_ref.shape
    pos = lax.broadcasted_iota(jnp.int32, shape, 0).astype(F32)
    lane = lax.broadcasted_iota(jnp.int32, shape, 1)
    idx = (lane % ROPE_HALF).astype(F32)
    inv = jnp.exp(idx * (-2.0 / ROPE_DIM) * jnp.log(jnp.float32(ROPE_THETA)))
    ang = pos * inv
    cos = jnp.cos(ang)
    sin = jnp.sin(ang)
    cos_ref[...] = jnp.where(lane < ROPE_DIM, cos, 1.0)
    sin_ref[...] = jnp.where(lane < ROPE_HALF, -sin, jnp.where(lane < ROPE_DIM, sin, 0.0))


def _rope_tables(seq):
    shp = jax.ShapeDtypeStruct((seq, HEAD_DIM), F32)
    return pl.pallas_call(_rope_table_kernel, out_shape=[shp, shp], name="rope_tables")()


def _attn_kernel(q0_ref, q1_ref, q2_ref, k_ref, v_ref, z_ref, cos_ref, sin_ref, o_ref,
                 qs_ref, ks_ref, vs_ref, m_ref, l_ref, acc_ref, bias_ref, *, seq, unroll):
    wb = SW_BLOCK
    dh = HEAD_DIM
    rows = 256
    pr = lax.broadcasted_iota(jnp.int32, (dh, dh), 0)
    pc = lax.broadcasted_iota(jnp.int32, (dh, dh), 1)
    swap = jnp.where(((pc < ROPE_HALF) & (pr == pc + ROPE_HALF))
                     | ((pc >= ROPE_HALF) & (pc < ROPE_DIM) & (pr == pc - ROPE_HALF)), 1.0, 0.0).astype(BF16)

    def rope(x, cos, sin):
        swapped = jnp.dot(x.astype(BF16), swap, preferred_element_type=F32)
        return x * cos + swapped * sin

    def rope_body(i, carry):
        r0 = pl.multiple_of(i * rows, rows)
        cos = cos_ref[pl.ds(r0, rows), :]
        sin = sin_ref[pl.ds(r0, rows), :]
        for g, q_ref in enumerate((q0_ref, q1_ref, q2_ref)):
            qs_ref[g, pl.ds(r0, rows), :] = rope(q_ref[pl.ds(r0, rows), :].astype(F32), cos, sin) * (dh ** -0.5)
        ks_ref[pl.ds(r0, rows), :] = rope(k_ref[pl.ds(r0, rows), :].astype(F32), cos, sin)
        vs_ref[pl.ds(r0, rows), :] = v_ref[pl.ds(r0, rows), :].astype(F32)
        return carry

    lax.fori_loop(0, seq // rows, rope_body, 0)

    delta = lax.broadcasted_iota(jnp.int32, (wb, 2 * wb), 1) - lax.broadcasted_iota(jnp.int32, (wb, 2 * wb), 0)
    bias_ref[0] = jnp.where((delta >= 0) & (delta <= wb), 0.0, NEG_BIG)
    bias_ref[1] = jnp.where(delta <= 0, 0.0, NEG_BIG)

    order = sorted(range(len(SW_DILATIONS)), key=lambda gi: -SW_DILATIONS[gi])
    for pos_in_order, g in enumerate(order):
        dil = SW_DILATIONS[g]
        first = pos_in_order == 0
        last = pos_in_order == len(order) - 1
        nb = max(seq // dil // wb, 1)
        has_prev = nb > 1
        kw = 2 * wb if has_prev else wb
        ones = jnp.ones((kw, dh), BF16)

        def block_body(it, carry, g=g, dil=dil, nb=nb, has_prev=has_prev, kw=kw, ones=ones,
                       first=first, last=last):
            def rows_of(ref, s, count):
                if dil == 1:
                    return ref[pl.ds(s, count), :]
                return ref[pl.ds(s, count, stride=dil), :]

            idxs = [it * unroll + u for u in range(unroll)]
            ns = [idx & (nb - 1) for idx in idxs]
            starts = [lax.shift_right_logical(idx, nb.bit_length() - 1) + n * (dil * wb) for idx, n in zip(idxs, ns)]
            if dil == 1:
                starts = [pl.multiple_of(s, wb) for s in starts]
            if has_prev:
                kstarts = [jnp.where(n > 0, s - dil * wb, s) for s, n in zip(starts, ns)]
                biases = [bias_ref[jnp.where(n > 0, 0, 1)] for n in ns]
            else:
                kstarts = starts
                biases = [bias_ref[1, :, 0:wb] for _ in ns]
            qs = [rows_of(qs_ref.at[g], s, wb).astype(BF16) for s in starts]
            kwins = [rows_of(ks_ref, s, kw).astype(BF16) for s in kstarts]
            vaugs = [jnp.concatenate([rows_of(vs_ref, s, kw).astype(BF16), ones], axis=1) for s in kstarts]
            scs = [lax.dot_general(q, kwin, _NT, preferred_element_type=F32) + bias
                   for bias, q, kwin in zip(biases, qs, kwins)]
            m_blks = [jnp.max(sc, axis=-1, keepdims=True) for sc in scs]
            ps = [jnp.exp(sc - m).astype(BF16) for sc, m in zip(scs, m_blks)]
            pvs = [jnp.dot(p, vaug, preferred_element_type=F32) for p, vaug in zip(ps, vaugs)]
            for s, m_blk, pv in zip(starts, m_blks, pvs):
                m_blk = jnp.broadcast_to(m_blk, (wb, dh))
                acc_blk = pv[:, :dh]
                l_blk = pv[:, dh:]
                sl = pl.ds(s, wb) if dil == 1 else pl.ds(s, wb, stride=dil)
                if first:
                    m_ref[sl, :] = m_blk
                    l_ref[sl, :] = l_blk
                    acc_ref[sl, :] = acc_blk
                    continue
                m_old = m_ref[sl, :]
                m_new = jnp.maximum(m_old, m_blk)
                a_old = jnp.exp(m_old - m_new)
                a_blk = jnp.exp(m_blk - m_new)
                l_new = a_old * l_ref[sl, :] + a_blk * l_blk
                acc_new = a_old * acc_ref[sl, :] + a_blk * acc_blk
                if last:
                    z = z_ref[sl, :].astype(F32)
                    o_ref[sl, :] = (acc_new / l_new * _silu(z)).astype(o_ref.dtype)
                else:
                    m_ref[sl, :] = m_new
                    l_ref[sl, :] = l_new
                    acc_ref[sl, :] = acc_new
            return carry

        lax.fori_loop(0, dil * nb // unroll, block_body, 0)


def _dilated_attention(proj_b, cos_tab, sin_tab, batch, seq, unroll=16):
    dh = HEAD_DIM
    nh = SW_HEADS
    ng = len(SW_DILATIONS)

    def col(block):
        return pl.BlockSpec((seq, dh), lambda b, h: (b, block * nh + h))

    tab = pl.BlockSpec((seq, dh), lambda b, h: (0, 0))
    return pl.pallas_call(
        functools.partial(_attn_kernel, seq=seq, unroll=unroll),
        grid=(batch, nh),
        in_specs=[col(0), col(1), col(2), col(ng), col(ng + 1), col(ng + 2), tab, tab],
        out_specs=pl.BlockSpec((seq, dh), lambda b, h: (b, h)),
        out_shape=jax.ShapeDtypeStruct((batch * seq, nh * dh), BF16),
        scratch_shapes=[pltpu.VMEM((ng, seq, dh), F32), pltpu.VMEM((seq, dh), F32), pltpu.VMEM((seq, dh), F32),
                        pltpu.VMEM((seq, dh), F32), pltpu.VMEM((seq, dh), F32), pltpu.VMEM((seq, dh), F32),
                        pltpu.VMEM((2, SW_BLOCK, 2 * SW_BLOCK), F32)],
        compiler_params=_params("arbitrary", "arbitrary"),
        name="dilated_attn",
    )(proj_b, proj_b, proj_b, proj_b, proj_b, proj_b, cos_tab, sin_tab)


def _shortconv_kernel(x_ref, wb_ref, wc_ref, wu_ref, wz_ref, cw_ref, cast_src_ref, o_ref, cast_out_ref, wq_ref, halo_ref,
                      *, steps_per_seq):
    i = pl.program_id(1)
    cast_out_ref[...] = cast_src_ref[...].astype(BF16)

    @pl.when(i == 0)
    def _():
        for g, w_ref in enumerate((wb_ref, wc_ref, wu_ref, wz_ref)):
            wq_ref[g] = w_ref[...].astype(BF16)

    x = x_ref[...]
    tm = x.shape[0]
    gate_c = jnp.dot(x, wq_ref[1], preferred_element_type=F32)
    u = jnp.dot(x, wq_ref[2], preferred_element_type=F32)
    cu = gate_c * u
    prev = jnp.where(i % steps_per_seq != 0, halo_ref[...], 0.0)
    halo_ref[...] = cu[tm - 8:, :]
    xx = jnp.concatenate([prev, cu], axis=0)
    conv = cw_ref[SC_CONV - 1:SC_CONV, :] * cu
    for t in range(SC_CONV - 1):
        lo = 8 - (SC_CONV - 1) + t
        conv = conv + cw_ref[t:t + 1, :] * xx[lo:lo + tm, :]
    gate_b = jnp.dot(x, wq_ref[0], preferred_element_type=F32)
    z = jnp.dot(x, wq_ref[3], preferred_element_type=F32)
    o_ref[...] = (gate_b * conv * _silu(z)).astype(o_ref.dtype)


def _shortconv_mix(hn, w_in, layer, conv_w_t, w_out, seq, tm=1024, tn=256):
    m, k = hn.shape
    nblk = SC_WIDTH // tn
    n_i = m // tm
    _, r, c = w_out.shape
    rows = r // (nblk * n_i)
    assert rows * nblk * n_i == r

    def wspec(g):
        return pl.BlockSpec((None, k, tn), lambda j, i: (layer, 0, g * nblk + j))

    return pl.pallas_call(
        functools.partial(_shortconv_kernel, steps_per_seq=seq // tm),
        grid=(nblk, n_i),
        in_specs=[pl.BlockSpec((tm, k), lambda j, i: (i, 0)), wspec(0), wspec(1), wspec(2), wspec(3),
                  pl.BlockSpec((None, SC_CONV, tn), lambda j, i: (layer, 0, j)),
                  pl.BlockSpec((None, rows, c), lambda j, i: (layer, j * n_i + i, 0))],
        out_specs=[pl.BlockSpec((tm, tn), lambda j, i: (i, j)),
                   pl.BlockSpec((rows, c), lambda j, i: (j * n_i + i, 0))],
        out_shape=[jax.ShapeDtypeStruct((m, SC_WIDTH), BF16), jax.ShapeDtypeStruct((r, c), BF16)],
        scratch_shapes=[pltpu.VMEM((4, k, tn), BF16), pltpu.VMEM((8, tn), F32)],
        compiler_params=_params("arbitrary", "arbitrary"),
        name="shortconv_mix",
    )(hn, w_in, w_in, w_in, w_in, conv_w_t, w_out)


def kernel(x, norm_w, hyb_w_in, dn_conv_w, dn_a_log, dn_dt_bias, dn_norm_w, hyb_w_out, sc_w_in, sc_conv_w,
           sc_w_out, final_norm_w):
    batch, seq, d = x.shape
    h = x.reshape(batch * seq, d)
    cos_tab, sin_tab = _rope_tables(seq)
    dn_conv_t = jnp.swapaxes(dn_conv_w, 1, 2)
    sc_conv_t = jnp.swapaxes(sc_conv_w, 1, 2)
    hyb_wt = jnp.swapaxes(hyb_w_in, 1, 2)
    hn = _rmsnorm(h, norm_w[0], BF16)
    for layer in range(DEPTH):
        li = layer // 2
        last = layer == DEPTH - 1
        next_norm = final_norm_w if last else norm_w[layer + 1]
        if layer % 2 == 0:
            proj_a, w_out = _matmul_nt(hn, hyb_wt, li, 0, A_WIDTH, BF16, cast_src=hyb_w_out)
            proj_b = _matmul_nt(hn, hyb_wt, li, B_OFF, B_WIDTH, BF16)
            gcol, grow = _gates(hn, hyb_wt, li, dn_a_log[li], dn_dt_bias[li], batch, seq)
            y_a = _deltanet(proj_a, dn_conv_t[li], gcol, grow, dn_norm_w[li], batch, seq)
            y_b = _dilated_attention(proj_b, cos_tab, sin_tab, batch, seq)
            outs = _out_proj_residual_norm([y_a, y_b], w_out, h, next_norm, last)
        else:
            y, w_out = _shortconv_mix(hn, sc_w_in, li, sc_conv_t, sc_w_out, seq)
            outs = _out_proj_residual_norm([y], w_out, h, next_norm, last)
        if last:
            return outs[0].reshape(batch, seq, d)
        h, hn = outs
```

```python
import functools

import jax
import jax.numpy as jnp
from jax import lax
from jax.experimental import pallas as pl
from jax.experimental.pallas import tpu as pltpu

F32 = jnp.float32
BF16 = jnp.bfloat16

DEPTH = 4
HEAD_DIM = 128
EPS = 1e-6
DN_QK_HEADS = 8
DN_V_HEADS = 16
DN_CONV = 4
DN_CHUNK = 128
SW_HEADS = 8
SW_DILATIONS = (1, 4, 16)
SW_BLOCK = 128
ROPE_THETA = 500000.0
ROPE_DIM = HEAD_DIM // 4
ROPE_HALF = ROPE_DIM // 2
SC_WIDTH = 3072
SC_CONV = 3

A_Q = DN_QK_HEADS * HEAD_DIM
A_V = DN_V_HEADS * HEAD_DIM
A_WIDTH = 2 * A_Q + 2 * A_V
GATE_OFF = A_WIDTH
B_OFF = A_WIDTH + 2 * DN_V_HEADS
B_WIDTH = (len(SW_DILATIONS) + 3) * SW_HEADS * HEAD_DIM

LANES = 128
VMEM_LIMIT = 56 * 1024 * 1024
NEG_BIG = -1e30

_NT = (((1,), (1,)), ((), ()))
_TN = (((0,), (0,)), ((), ()))


def _params(*sem):
    return pltpu.CompilerParams(dimension_semantics=sem, vmem_limit_bytes=VMEM_LIMIT)


def _rmsnorm_kernel(x_ref, w_ref, o_ref):
    x = x_ref[...]
    ms = jnp.mean(x * x, axis=-1, keepdims=True)
    o_ref[...] = (x * lax.rsqrt(ms + EPS) * w_ref[...]).astype(o_ref.dtype)


def _rmsnorm(x, w, out_dtype, tm=1024):
    m, d = x.shape
    return pl.pallas_call(
        _rmsnorm_kernel,
        grid=(m // tm,),
        in_specs=[pl.BlockSpec((tm, d), lambda i: (i, 0)),
                  pl.BlockSpec((1, d), lambda i: (0, 0))],
        out_specs=pl.BlockSpec((tm, d), lambda i: (i, 0)),
        out_shape=jax.ShapeDtypeStruct((m, d), out_dtype),
        compiler_params=_params("arbitrary"),
        name="rmsnorm",
    )(x, w.reshape(1, d))


def _mm_nt_kernel(x_ref, w_ref, wx_ref, *rest, shift, with_cast):
    if with_cast:
        cast_src_ref, o_ref, cast_out_ref, wb_ref = rest
        cast_out_ref[...] = cast_src_ref[...].astype(BF16)
    else:
        o_ref, wb_ref = rest

    @pl.when(pl.program_id(1) == 0)
    def _():
        tn = wb_ref.shape[0]
        wb_ref[0:tn - shift, :] = w_ref[shift:tn, :].astype(BF16)
        if shift:
            wb_ref[tn - shift:tn, :] = wx_ref[...].astype(BF16)

    o_ref[...] = lax.dot_general(x_ref[...], wb_ref[...], _NT, preferred_element_type=F32).astype(o_ref.dtype)


def _matmul_nt(x, wt, layer, row0, n_rows, out_dtype, cast_src=None, tm=1024, tn=1024, shift_rows=32):
    m, k = x.shape
    shift = row0 % tn
    assert shift in (0, shift_rows) and tn % shift_rows == 0
    base = row0 // tn
    n_j, n_i = n_rows // tn, m // tm
    in_specs = [pl.BlockSpec((tm, k), lambda j, i: (i, 0)),
                pl.BlockSpec((None, tn, k), lambda j, i: (layer, base + j, 0)),
                pl.BlockSpec((None, shift_rows, k), lambda j, i: (layer, (base + j + 1) * (tn // shift_rows), 0))]
    out_specs = [pl.BlockSpec((tm, tn), lambda j, i: (i, j))]
    out_shape = [jax.ShapeDtypeStruct((m, n_rows), out_dtype)]
    args = [x, wt, wt]
    if cast_src is not None:
        _, r, c = cast_src.shape
        rows = r // (n_j * n_i)
        assert rows * n_j * n_i == r
        in_specs.append(pl.BlockSpec((None, rows, c), lambda j, i: (layer, j * n_i + i, 0)))
        out_specs.append(pl.BlockSpec((rows, c), lambda j, i: (j * n_i + i, 0)))
        out_shape.append(jax.ShapeDtypeStruct((r, c), BF16))
        args.append(cast_src)
    outs = pl.pallas_call(
        functools.partial(_mm_nt_kernel, shift=shift, with_cast=cast_src is not None),
        grid=(n_j, n_i),
        in_specs=in_specs,
        out_specs=out_specs,
        out_shape=out_shape,
        scratch_shapes=[pltpu.VMEM((tn, k), BF16)],
        compiler_params=_params("arbitrary", "arbitrary"),
        name="in_proj",
    )(*args)
    return outs if cast_src is not None else outs[0]


def _out_proj_kernel(*refs, k_sizes, emit_h):
    nx = len(k_sizes)
    x_refs = refs[:nx]
    w_ref, h_ref, nw_ref = refs[nx:nx + 3]
    out_refs = refs[nx + 3:]
    acc = h_ref[...]
    off = 0
    for x_ref, ks in zip(x_refs, k_sizes):
        acc = acc + jnp.dot(x_ref[...], w_ref[off:off + ks, :], preferred_element_type=F32)
        off += ks
    if emit_h:
        out_refs[0][...] = acc
    hn_ref = out_refs[-1]
    ms = jnp.mean(acc * acc, axis=-1, keepdims=True)
    hn_ref[...] = (acc * lax.rsqrt(ms + EPS) * nw_ref[...]).astype(hn_ref.dtype)


def _out_proj_residual_norm(xs, w_bf16, h, norm_w, last, tm=512):
    m, n = h.shape
    k_sizes = tuple(x.shape[1] for x in xs)
    k = sum(k_sizes)
    row = pl.BlockSpec((tm, n), lambda i: (i, 0))
    if last:
        out_shape = [jax.ShapeDtypeStruct((m, n), F32)]
    else:
        out_shape = [jax.ShapeDtypeStruct((m, n), F32), jax.ShapeDtypeStruct((m, n), BF16)]
    outs = pl.pallas_call(
        functools.partial(_out_proj_kernel, k_sizes=k_sizes, emit_h=not last),
        grid=(m // tm,),
        in_specs=[pl.BlockSpec((tm, ks), lambda i: (i, 0)) for ks in k_sizes]
        + [pl.BlockSpec((k, n), lambda i: (0, 0), pipeline_mode=pl.Buffered(1)), row,
           pl.BlockSpec((1, n), lambda i: (0, 0))],
        out_specs=[row] * len(out_shape),
        out_shape=out_shape,
        compiler_params=_params("arbitrary"),
        name="out_proj",
    )(*xs, w_bf16, h, norm_w.reshape(1, n))
    return outs


def _softplus(x):
    return jnp.maximum(x, 0.0) + jnp.log1p(jnp.exp(-jnp.abs(x)))


def _gates_kernel(x_ref, w_ref, alog_ref, dtb_ref, col_ref, row_ref, ws_ref, *, tm):
    nh = DN_V_HEADS
    c = DN_CHUNK

    @pl.when(pl.program_id(0) == 0)
    def _():
        ws_ref[...] = w_ref[...].astype(BF16)

    ri = lax.broadcasted_iota(jnp.int32, (c, c), 0)
    ci = lax.broadcasted_iota(jnp.int32, (c, c), 1)
    lower = jnp.where(ci <= ri, 1.0, 0.0).astype(BF16)

    p = lax.dot_general(x_ref[...], ws_ref[...], _NT, preferred_element_type=F32)
    beta = jax.nn.sigmoid(p[:, :nh])
    g = -jnp.exp(alog_ref[...]) * _softplus(p + dtb_ref[...])
    hi = g.astype(BF16)
    r1 = g - hi.astype(F32)
    mid = r1.astype(BF16)
    lo = (r1 - mid.astype(F32)).astype(BF16)
    for ch in range(tm // c):
        rows = slice(ch * c, (ch + 1) * c)
        gc = sum(jnp.dot(lower, part[rows, :], preferred_element_type=F32) for part in (hi, mid, lo))
        for hq in range(DN_QK_HEADS):
            col_ref[hq, rows, 0:2] = beta[rows, 2 * hq:2 * hq + 2]
            col_ref[hq, rows, 2:4] = gc[:, nh + 2 * hq:nh + 2 * hq + 2]
        row_ref[ch] = gc.T[nh:2 * nh, :]


def _gates(hn, wt, layer, a_log, dt_bias, batch, seq, tm=512):
    m, k = hn.shape
    nh = DN_V_HEADS
    nch = seq // DN_CHUNK
    assert GATE_OFF % LANES == 0
    steps = seq // tm

    def at_alpha_lanes(v):
        return jnp.zeros((1, LANES), F32).at[0, nh:2 * nh].set(v)

    col, row = pl.pallas_call(
        functools.partial(_gates_kernel, tm=tm),
        grid=(m // tm,),
        in_specs=[pl.BlockSpec((tm, k), lambda i: (i, 0)),
                  pl.BlockSpec((None, LANES, k), lambda i: (layer, GATE_OFF // LANES, 0)),
                  pl.BlockSpec((1, LANES), lambda i: (0, 0)),
                  pl.BlockSpec((1, LANES), lambda i: (0, 0))],
        out_specs=[pl.BlockSpec((None, DN_QK_HEADS, tm, 4), lambda i: (i // steps, 0, i % steps, 0)),
                   pl.BlockSpec((None, tm // DN_CHUNK, nh, DN_CHUNK), lambda i: (i // steps, i % steps, 0, 0))],
        out_shape=[jax.ShapeDtypeStruct((batch, DN_QK_HEADS, seq, 4), F32),
                   jax.ShapeDtypeStruct((batch, nch, nh, DN_CHUNK), F32)],
        scratch_shapes=[pltpu.VMEM((LANES, k), BF16)],
        compiler_params=_params("arbitrary"),
        name="dn_gates",
    )(hn, wt, at_alpha_lanes(a_log), at_alpha_lanes(dt_bias))
    return col, row


def _conv_silu(pad_ref, col0, ncols, cw_ref, t0, width):
    c = DN_CHUNK
    acc = None
    for i in range(width):
        parts = [pad_ref[col0 // HEAD_DIM + p, pl.ds(t0 + (8 - (width - 1) + i), c, stride=1), :]
                 for p in range(ncols // HEAD_DIM)]
        x = parts[0] if len(parts) == 1 else jnp.concatenate(parts, axis=1)
        term = cw_ref[i:i + 1, :] * x
        acc = term if acc is None else acc + term
    return _silu(acc)


def _l2norm(x):
    return x * lax.rsqrt(jnp.sum(x * x, axis=-1, keepdims=True) + EPS)


def _silu(x):
    return x / (1.0 + jnp.exp(-x))


def _unit_lower_inverse(mats):
    c = mats[0].shape[0]
    ri = lax.broadcasted_iota(jnp.int32, (c, c), 0)
    ci = lax.broadcasted_iota(jnp.int32, (c, c), 1)
    x = ri ^ ci
    zero = jnp.zeros((), BF16)
    eye = jnp.where(ri == ci, 1.0, 0.0).astype(BF16)
    ts = [eye - jnp.where(x < 2, a, zero) for a in mats]
    b = 2
    while b < c:
        level = (x >= b) & (x < 2 * b)
        ls = [jnp.where(level, a, zero) for a in mats]
        xs = [jnp.dot(t, l, preferred_element_type=F32).astype(BF16) for t, l in zip(ts, ls)]
        yield
        ts = [t - jnp.dot(xm, t, preferred_element_type=F32).astype(BF16) for t, xm in zip(ts, xs)]
        yield
        b *= 2
    return ts


def _deltanet_kernel(q_ref, k_ref, v_ref, z_ref, cwq_ref, cwk_ref, cwv_ref, gcol_ref, grow_ref, nw_ref,
                     o_ref, state_ref, gq_ref, cc_ref, op_ref, pad_ref, *, seq, unroll, scan_every):
    c = DN_CHUNK
    dk = HEAD_DIM
    nch = seq // c
    hq = pl.program_id(1)
    ri = lax.broadcasted_iota(jnp.int32, (c, c), 0)
    ci = lax.broadcasted_iota(jnp.int32, (c, c), 1)
    causal = ri >= ci
    strict = ri > ci

    pad_ref[:, 0:8, :] = jnp.zeros((4, 8, dk), F32)

    def pad_body(i, carry):
        r0 = pl.multiple_of(i * c, c)
        pad_ref[0, pl.ds(8 + r0, c), :] = q_ref[pl.ds(r0, c), :].astype(F32)
        pad_ref[1, pl.ds(8 + r0, c), :] = k_ref[pl.ds(r0, c), :].astype(F32)
        pad_ref[2, pl.ds(8 + r0, c), :] = v_ref[pl.ds(r0, c), 0:dk].astype(F32)
        pad_ref[3, pl.ds(8 + r0, c), :] = v_ref[pl.ds(r0, c), dk:2 * dk].astype(F32)
        return carry

    lax.fori_loop(0, nch, pad_body, 0)

    def prep_stages(i):
        chunks = [i * unroll + uu for uu in range(unroll)]
        t0s = [pl.multiple_of(n * c, c) for n in chunks]
        qs = [_l2norm(_conv_silu(pad_ref, 0, dk, cwq_ref, t0, DN_CONV)) * (dk ** -0.5) for t0 in t0s]
        ks = [_l2norm(_conv_silu(pad_ref, dk, dk, cwk_ref, t0, DN_CONV)) for t0 in t0s]
        v2s = [_conv_silu(pad_ref, 2 * dk, 2 * dk, cwv_ref, t0, DN_CONV) for t0 in t0s]
        kbs = [k.astype(BF16) for k in ks]
        kks = [lax.dot_general(kb, kb, _NT, preferred_element_type=F32) for kb in kbs]
        qks = [lax.dot_general(q.astype(BF16), kb, _NT, preferred_element_type=F32) for q, kb in zip(qs, kbs)]
        yield
        gcols = [gcol_ref[pl.ds(t0, c), :] for t0 in t0s]
        chains = [(u, j) for u in range(unroll) for j in range(2)]
        betas = [gcols[u][:, j:j + 1] for u, j in chains]
        gcs = [gcols[u][:, 2 + j:3 + j] for u, j in chains]
        grs = [grow_ref[chunks[u], pl.ds(2 * hq + j, 1), :] for u, j in chains]
        decays = [jnp.exp(jnp.where(causal, gc - gr, NEG_BIG)) for gc, gr in zip(gcs, grs)]
        mats = [jnp.where(strict, beta * kks[u] * decay, 0.0).astype(BF16)
                for (u, j), beta, decay in zip(chains, betas, decays)]
        ts = yield from _unit_lower_inverse(mats)
        egs = [jnp.exp(gc) for gc in gcs]
        rhss = [jnp.concatenate([v2s[u][:, j * dk:(j + 1) * dk] * beta, ks[u] * (beta * eg)], axis=1)
                for (u, j), beta, eg in zip(chains, betas, egs)]
        uws = [jnp.dot(t, rhs.astype(BF16), preferred_element_type=F32).astype(BF16)
               for t, rhs in zip(ts, rhss)]
        yield
        kds = [(ks[u] * jnp.exp(gc[c - 1:c, :] - gc)).astype(BF16) for (u, j), gc in zip(chains, gcs)]
        cgs = [lax.dot_general(kd, uw, _TN, preferred_element_type=F32) for kd, uw in zip(kds, uws)]
        yield
        oxs = [jnp.dot((qks[u] * decay).astype(BF16), uw, preferred_element_type=F32)
               for (u, j), decay, uw in zip(chains, decays, uws)]
        yield
        for (u, j), cg, ox, eg in zip(chains, cgs, oxs, egs):
            n = chunks[u]
            gq_ref[n, j, 0:c, :] = cg[:, dk:].astype(BF16)
            gq_ref[n, j, c:2 * c, :] = (qs[u] * eg - ox[:, dk:]).astype(BF16)
            cc_ref[n, j] = cg[:, :dk]
            op_ref[n, j] = ox[:, :dk]

    def scan_steps(i):
        for uu in range(unroll):
            n = i * unroll + uu
            t0 = pl.multiple_of(n * c, c)
            z2 = z_ref[pl.ds(t0, c), :].astype(F32)
            for j in range(2):
                s = state_ref[j]
                prod = jnp.dot(gq_ref[n, j], s.astype(BF16), preferred_element_type=F32)
                gr = grow_ref[n, pl.ds(2 * hq + j, 1), :]
                state_ref[j] = s * jnp.exp(gr[:, c - 1:c]) - prod[:c] + cc_ref[n, j]
                o = prod[c:] + op_ref[n, j]
                o = o * lax.rsqrt(jnp.mean(o * o, axis=-1, keepdims=True) + EPS) * nw_ref[...]
                z = z2[:, j * dk:(j + 1) * dk]
                o_ref[pl.ds(t0, c), j * dk:(j + 1) * dk] = (o * _silu(z)).astype(o_ref.dtype)
            yield

    def run_interleaved(prep, scan, every):
        for count, _ in enumerate(prep, start=1):
            if count % every == 0:
                next(scan, None)
        for _ in scan:
            pass

    state_ref[...] = jnp.zeros_like(state_ref)
    groups = nch // unroll
    for _ in prep_stages(0):
        pass

    def pipelined_body(i, carry):
        run_interleaved(prep_stages(i), scan_steps(i - 1), every=scan_every)
        return carry

    lax.fori_loop(1, groups, pipelined_body, 0)
    for _ in scan_steps(groups - 1):
        pass


def _deltanet(proj_a, conv_w_t, gcol, grow, norm_w, batch, seq, unroll=8, scan_every=1):
    dk = HEAD_DIM
    nq = DN_QK_HEADS
    c = DN_CHUNK
    nch = seq // c
    return pl.pallas_call(
        functools.partial(_deltanet_kernel, seq=seq, unroll=unroll, scan_every=scan_every),
        grid=(batch, nq),
        in_specs=[pl.BlockSpec((seq, dk), lambda b, h: (b, h)),
                  pl.BlockSpec((seq, dk), lambda b, h: (b, nq + h)),
                  pl.BlockSpec((seq, 2 * dk), lambda b, h: (b, nq + h)),
                  pl.BlockSpec((seq, 2 * dk), lambda b, h: (b, 2 * nq + h)),
                  pl.BlockSpec((DN_CONV, dk), lambda b, h: (0, h)),
                  pl.BlockSpec((DN_CONV, dk), lambda b, h: (0, nq + h)),
                  pl.BlockSpec((DN_CONV, 2 * dk), lambda b, h: (0, nq + h)),
                  pl.BlockSpec((None, None, seq, 4), lambda b, h: (b, h, 0, 0)),
                  pl.BlockSpec((None, nch, DN_V_HEADS, c), lambda b, h: (b, 0, 0, 0)),
                  pl.BlockSpec((1, dk), lambda b, h: (0, 0))],
        out_specs=pl.BlockSpec((seq, 2 * dk), lambda b, h: (b, h)),
        out_shape=jax.ShapeDtypeStruct((batch * seq, A_V), BF16),
        scratch_shapes=[pltpu.VMEM((2, dk, dk), F32),
                        pltpu.VMEM((nch, 2, 2 * c, dk), BF16),
                        pltpu.VMEM((nch, 2, dk, dk), F32),
                        pltpu.VMEM((nch, 2, c, dk), F32),
                        pltpu.VMEM((4, 8 + seq, dk), F32)],
        compiler_params=_params("arbitrary", "arbitrary"),
        name="deltanet",
    )(proj_a, proj_a, proj_a, proj_a, conv_w_t, conv_w_t, conv_w_t, gcol, grow, norm_w.reshape(1, dk))


def _rope_table_kernel(cos_ref, sin_ref):
    shape = cos_ref.shape
    pos = lax.broadcasted_iota(jnp.int32, shape, 0).astype(F32)
    lane = lax.broadcasted_iota(jnp.int32, shape, 1)
    idx = (lane % ROPE_HALF).astype(F32)
    inv = jnp.exp(idx * (-2.0 / ROPE_DIM) * jnp.log(jnp.float32(ROPE_THETA)))
    ang = pos * inv
    cos = jnp.cos(ang)
    sin = jnp.sin(ang)
    cos_ref[...] = jnp.where(lane < ROPE_DIM, cos, 1.0)
    sin_ref[...] = jnp.where(lane < ROPE_HALF, -sin, jnp.where(lane < ROPE_DIM, sin, 0.0))


def _rope_tables(seq):
    shp = jax.ShapeDtypeStruct((seq, HEAD_DIM), F32)
    return pl.pallas_call(_rope_table_kernel, out_shape=[shp, shp], name="rope_tables")()


def _attn_kernel(q0_ref, q1_ref, q2_ref, k_ref, v_ref, z_ref, cos_ref, sin_ref, o_ref,
                 qs_ref, ks_ref, vs_ref, m_ref, l_ref, acc_ref, bias_ref, *, seq, unroll):
    wb = SW_BLOCK
    dh = HEAD_DIM
    rows = 1024
    pr = lax.broadcasted_iota(jnp.int32, (dh, dh), 0)
    pc = lax.broadcasted_iota(jnp.int32, (dh, dh), 1)
    swap = jnp.where(((pc < ROPE_HALF) & (pr == pc + ROPE_HALF))
                     | ((pc >= ROPE_HALF) & (pc < ROPE_DIM) & (pr == pc - ROPE_HALF)), 1.0, 0.0).astype(BF16)

    def rope(x, cos, sin):
        swapped = jnp.dot(x.astype(BF16), swap, preferred_element_type=F32)
        return x * cos + swapped * sin

    def rope_body(i, carry):
        r0 = pl.multiple_of(i * rows, rows)
        cos = cos_ref[pl.ds(r0, rows), :]
        sin = sin_ref[pl.ds(r0, rows), :]
        for g, q_ref in enumerate((q0_ref, q1_ref, q2_ref)):
            qs_ref[g, pl.ds(r0, rows), :] = rope(q_ref[pl.ds(r0, rows), :].astype(F32), cos, sin) * (dh ** -0.5)
        ks_ref[pl.ds(r0, rows), :] = rope(k_ref[pl.ds(r0, rows), :].astype(F32), cos, sin)
        vs_ref[pl.ds(r0, rows), :] = v_ref[pl.ds(r0, rows), :].astype(F32)
        return carry

    lax.fori_loop(0, seq // rows, rope_body, 0)

    delta = lax.broadcasted_iota(jnp.int32, (wb, 2 * wb), 1) - lax.broadcasted_iota(jnp.int32, (wb, 2 * wb), 0)
    bias_ref[0] = jnp.where((delta >= 0) & (delta <= wb), 0.0, NEG_BIG)
    bias_ref[1] = jnp.where(delta <= 0, 0.0, NEG_BIG)

    order = sorted(range(len(SW_DILATIONS)), key=lambda gi: -SW_DILATIONS[gi])
    for pos_in_order, g in enumerate(order):
        dil = SW_DILATIONS[g]
        first = pos_in_order == 0
        last = pos_in_order == len(order) - 1
        nb = max(seq // dil // wb, 1)
        has_prev = nb > 1
        kw = 2 * wb if has_prev else wb
        ones = jnp.ones((kw, dh), BF16)

        def block_body(it, carry, g=g, dil=dil, nb=nb, has_prev=has_prev, kw=kw, ones=ones,
                       first=first, last=last):
            def rows_of(ref, s, count):
                if dil == 1:
                    return ref[pl.ds(s, count), :]
                return ref[pl.ds(s, count, stride=dil), :]

            idxs = [it * unroll + u for u in range(unroll)]
            ns = [idx & (nb - 1) for idx in idxs]
            starts = [lax.shift_right_logical(idx, nb.bit_length() - 1) + n * (dil * wb) for idx, n in zip(idxs, ns)]
            if dil == 1:
                starts = [pl.multiple_of(s, wb) for s in starts]
            if has_prev:
                kstarts = [jnp.where(n > 0, s - dil * wb, s) for s, n in zip(starts, ns)]
                biases = [bias_ref[jnp.where(n > 0, 0, 1)] for n in ns]
            else:
                kstarts = starts
                biases = [bias_ref[1, :, 0:wb] for _ in ns]
            qs = [rows_of(qs_ref.at[g], s, wb).astype(BF16) for s in starts]
            kwins = [rows_of(ks_ref, s, kw).astype(BF16) for s in kstarts]
            vaugs = [jnp.concatenate([rows_of(vs_ref, s, kw).astype(BF16), ones], axis=1) for s in kstarts]
            scs = [lax.dot_general(q, kwin, _NT, preferred_element_type=F32) + bias
                   for bias, q, kwin in zip(biases, qs, kwins)]
            m_blks = [jnp.max(sc, axis=-1, keepdims=True) for sc in scs]
            ps = [jnp.exp(sc - m).astype(BF16) for sc, m in zip(scs, m_blks)]
            pvs = [jnp.dot(p, vaug, preferred_element_type=F32) for p, vaug in zip(ps, vaugs)]
            for s, m_blk, pv in zip(starts, m_blks, pvs):
                m_blk = jnp.broadcast_to(m_blk, (wb, dh))
                acc_blk = pv[:, :dh]
                l_blk = pv[:, dh:]
                sl = pl.ds(s, wb) if dil == 1 else pl.ds(s, wb, stride=dil)
                if first:
                    m_ref[sl, :] = m_blk
                    l_ref[sl, :] = l_blk
                    acc_ref[sl, :] = acc_blk
                    continue
                m_old = m_ref[sl, :]
                m_new = jnp.maximum(m_old, m_blk)
                a_old = jnp.exp(m_old - m_new)
                a_blk = jnp.exp(m_blk - m_new)
                l_new = a_old * l_ref[sl, :] + a_blk * l_blk
                acc_new = a_old * acc_ref[sl, :] + a_blk * acc_blk
                if last:
                    z = z_ref[sl, :].astype(F32)
                    o_ref[sl, :] = (acc_new / l_new * _silu(z)).astype(o_ref.dtype)
                else:
                    m_ref[sl, :] = m_new
                    l_ref[sl, :] = l_new
                    acc_ref[sl, :] = acc_new
            return carry

        lax.fori_loop(0, dil * nb // unroll, block_body, 0)


def _dilated_attention(proj_b, cos_tab, sin_tab, batch, seq, unroll=16):
    dh = HEAD_DIM
    nh = SW_HEADS
    ng = len(SW_DILATIONS)

    def col(block):
        return pl.BlockSpec((seq, dh), lambda b, h: (b, block * nh + h))

    tab = pl.BlockSpec((seq, dh), lambda b, h: (0, 0))
    return pl.pallas_call(
        functools.partial(_attn_kernel, seq=seq, unroll=unroll),
        grid=(batch, nh),
        in_specs=[col(0), col(1), col(2), col(ng), col(ng + 1), col(ng + 2), tab, tab],
        out_specs=pl.BlockSpec((seq, dh), lambda b, h: (b, h)),
        out_shape=jax.ShapeDtypeStruct((batch * seq, nh * dh), BF16),
        scratch_shapes=[pltpu.VMEM((ng, seq, dh), F32), pltpu.VMEM((seq, dh), F32), pltpu.VMEM((seq, dh), F32),
                        pltpu.VMEM((seq, dh), F32), pltpu.VMEM((seq, dh), F32), pltpu.VMEM((seq, dh), F32),
                        pltpu.VMEM((2, SW_BLOCK, 2 * SW_BLOCK), F32)],
        compiler_params=_params("arbitrary", "arbitrary"),
        name="dilated_attn",
    )(proj_b, proj_b, proj_b, proj_b, proj_b, proj_b, cos_tab, sin_tab)


def _shortconv_kernel(x_ref, wb_ref, wc_ref, wu_ref, wz_ref, cw_ref, cast_src_ref, o_ref, cast_out_ref, wq_ref, halo_ref,
                      *, steps_per_seq):
    i = pl.program_id(1)
    cast_out_ref[...] = cast_src_ref[...].astype(BF16)

    @pl.when(i == 0)
    def _():
        for g, w_ref in enumerate((wb_ref, wc_ref, wu_ref, wz_ref)):
            wq_ref[g] = w_ref[...].astype(BF16)

    x = x_ref[...]
    tm = x.shape[0]
    gate_c = jnp.dot(x, wq_ref[1], preferred_element_type=F32)
    u = jnp.dot(x, wq_ref[2], preferred_element_type=F32)
    cu = gate_c * u
    prev = jnp.where(i % steps_per_seq != 0, halo_ref[...], 0.0)
    halo_ref[...] = cu[tm - 8:, :]
    xx = jnp.concatenate([prev, cu], axis=0)
    conv = cw_ref[SC_CONV - 1:SC_CONV, :] * cu
    for t in range(SC_CONV - 1):
        lo = 8 - (SC_CONV - 1) + t
        conv = conv + cw_ref[t:t + 1, :] * xx[lo:lo + tm, :]
    gate_b = jnp.dot(x, wq_ref[0], preferred_element_type=F32)
    z = jnp.dot(x, wq_ref[3], preferred_element_type=F32)
    o_ref[...] = (gate_b * conv * _silu(z)).astype(o_ref.dtype)


def _shortconv_mix(hn, w_in, layer, conv_w_t, w_out, seq, tm=1024, tn=256):
    m, k = hn.shape
    nblk = SC_WIDTH // tn
    n_i = m // tm
    _, r, c = w_out.shape
    rows = r // (nblk * n_i)
    assert rows * nblk * n_i == r

    def wspec(g):
        return pl.BlockSpec((None, k, tn), lambda j, i: (layer, 0, g * nblk + j))

    return pl.pallas_call(
        functools.partial(_shortconv_kernel, steps_per_seq=seq // tm),
        grid=(nblk, n_i),
        in_specs=[pl.BlockSpec((tm, k), lambda j, i: (i, 0)), wspec(0), wspec(1), wspec(2), wspec(3),
                  pl.BlockSpec((None, SC_CONV, tn), lambda j, i: (layer, 0, j)),
                  pl.BlockSpec((None, rows, c), lambda j, i: (layer, j * n_i + i, 0))],
        out_specs=[pl.BlockSpec((tm, tn), lambda j, i: (i, j)),
                   pl.BlockSpec((rows, c), lambda j, i: (j * n_i + i, 0))],
        out_shape=[jax.ShapeDtypeStruct((m, SC_WIDTH), BF16), jax.ShapeDtypeStruct((r, c), BF16)],
        scratch_shapes=[pltpu.VMEM((4, k, tn), BF16), pltpu.VMEM((8, tn), F32)],
        compiler_params=_params("arbitrary", "arbitrary"),
        name="shortconv_mix",
    )(hn, w_in, w_in, w_in, w_in, conv_w_t, w_out)


def kernel(x, norm_w, hyb_w_in, dn_conv_w, dn_a_log, dn_dt_bias, dn_norm_w, hyb_w_out, sc_w_in, sc_conv_w,
           sc_w_out, final_norm_w):
    batch, seq, d = x.shape
    h = x.reshape(batch * seq, d)
    cos_tab, sin_tab = _rope_tables(seq)
    dn_conv_t = jnp.swapaxes(dn_conv_w, 1, 2)
    sc_conv_t = jnp.swapaxes(sc_conv_w, 1, 2)
    hyb_wt = jnp.swapaxes(hyb_w_in, 1, 2)
    hn = _rmsnorm(h, norm_w[0], BF16)
    for layer in range(DEPTH):
        li = layer // 2
        last = layer == DEPTH - 1
        next_norm = final_norm_w if last else norm_w[layer + 1]
        if layer % 2 == 0:
            proj_a, w_out = _matmul_nt(hn, hyb_wt, li, 0, A_WIDTH, BF16, cast_src=hyb_w_out)
            proj_b = _matmul_nt(hn, hyb_wt, li, B_OFF, B_WIDTH, BF16)
            gcol, grow = _gates(hn, hyb_wt, li, dn_a_log[li], dn_dt_bias[li], batch, seq)
            y_a = _deltanet(proj_a, dn_conv_t[li], gcol, grow, dn_norm_w[li], batch, seq)
            y_b = _dilated_attention(proj_b, cos_tab, sin_tab, batch, seq)
            outs = _out_proj_residual_norm([y_a, y_b], w_out, h, next_norm, last)
        else:
            y, w_out = _shortconv_mix(hn, sc_w_in, li, sc_conv_t, sc_w_out, seq)
            outs = _out_proj_residual_norm([y], w_out, h, next_norm, last)
        if last:
            return outs[0].reshape(batch, seq, d)
        h, hn = outs
```

```python
import functools

import jax
import jax.numpy as jnp
from jax import lax
from jax.experimental import pallas as pl
from jax.experimental.pallas import tpu as pltpu

F32 = jnp.float32
BF16 = jnp.bfloat16

DEPTH = 4
HEAD_DIM = 128
EPS = 1e-6
DN_QK_HEADS = 8
DN_V_HEADS = 16
DN_CONV = 4
DN_CHUNK = 128
SW_HEADS = 8
SW_DILATIONS = (1, 4, 16)
SW_BLOCK = 128
ROPE_THETA = 500000.0
ROPE_DIM = HEAD_DIM // 4
ROPE_HALF = ROPE_DIM // 2
SC_WIDTH = 3072
SC_CONV = 3

A_Q = DN_QK_HEADS * HEAD_DIM
A_V = DN_V_HEADS * HEAD_DIM
A_WIDTH = 2 * A_Q + 2 * A_V
GATE_OFF = A_WIDTH
B_OFF = A_WIDTH + 2 * DN_V_HEADS
B_WIDTH = (len(SW_DILATIONS) + 3) * SW_HEADS * HEAD_DIM

LANES = 128
VMEM_LIMIT = 56 * 1024 * 1024
NEG_BIG = -1e30

_NT = (((1,), (1,)), ((), ()))
_TN = (((0,), (0,)), ((), ()))


def _params(*sem):
    return pltpu.CompilerParams(dimension_semantics=sem, vmem_limit_bytes=VMEM_LIMIT)


def _mm_nt_kernel(x_ref, w_ref, wx_ref, *rest, shift, with_cast):
    if with_cast:
        cast_src_ref, o_ref, cast_out_ref, wb_ref = rest
        cast_out_ref[...] = cast_src_ref[...].astype(BF16)
    else:
        o_ref, wb_ref = rest

    @pl.when(pl.program_id(1) == 0)
    def _():
        tn = wb_ref.shape[0]
        wb_ref[0:tn - shift, :] = w_ref[shift:tn, :].astype(BF16)
        if shift:
            wb_ref[tn - shift:tn, :] = wx_ref[...].astype(BF16)

    o_ref[...] = lax.dot_general(x_ref[...], wb_ref[...], _NT, preferred_element_type=F32).astype(o_ref.dtype)


def _matmul_nt(x, wt, layer, row0, n_rows, out_dtype, cast_src=None, tm=1024, tn=1024, shift_rows=32):
    m, k = x.shape
    shift = row0 % tn
    assert shift in (0, shift_rows) and tn % shift_rows == 0
    base = row0 // tn
    n_j, n_i = n_rows // tn, m // tm
    in_specs = [pl.BlockSpec((tm, k), lambda j, i: (i, 0)),
                pl.BlockSpec((None, tn, k), lambda j, i: (layer, base + j, 0)),
                pl.BlockSpec((None, shift_rows, k), lambda j, i: (layer, (base + j + 1) * (tn // shift_rows), 0))]
    out_specs = [pl.BlockSpec((tm, tn), lambda j, i: (i, j))]
    out_shape = [jax.ShapeDtypeStruct((m, n_rows), out_dtype)]
    args = [x, wt, wt]
    if cast_src is not None:
        _, r, c = cast_src.shape
        rows = r // (n_j * n_i)
        assert rows * n_j * n_i == r
        in_specs.append(pl.BlockSpec((None, rows, c), lambda j, i: (layer, j * n_i + i, 0)))
        out_specs.append(pl.BlockSpec((rows, c), lambda j, i: (j * n_i + i, 0)))
        out_shape.append(jax.ShapeDtypeStruct((r, c), BF16))
        args.append(cast_src)
    outs = pl.pallas_call(
        functools.partial(_mm_nt_kernel, shift=shift, with_cast=cast_src is not None),
        grid=(n_j, n_i),
        in_specs=in_specs,
        out_specs=out_specs,
        out_shape=out_shape,
        scratch_shapes=[pltpu.VMEM((tn, k), BF16)],
        compiler_params=_params("arbitrary", "arbitrary"),
        name="in_proj",
    )(*args)
    return outs if cast_src is not None else outs[0]


def _out_proj_kernel(*refs, k_sizes, emit_h):
    nx = len(k_sizes)
    x_refs = refs[:nx]
    w_ref, h_ref, nw_ref = refs[nx:nx + 3]
    out_refs = refs[nx + 3:]
    acc = h_ref[...]
    off = 0
    for x_ref, ks in zip(x_refs, k_sizes):
        acc = acc + jnp.dot(x_ref[...], w_ref[off:off + ks, :], preferred_element_type=F32)
        off += ks
    if emit_h:
        out_refs[0][...] = acc
    hn_ref = out_refs[-1]
    ms = jnp.mean(acc * acc, axis=-1, keepdims=True)
    hn_ref[...] = (acc * lax.rsqrt(ms + EPS) * nw_ref[...]).astype(hn_ref.dtype)


def _out_proj_residual_norm(xs, w_bf16, h, norm_w, last, tm=512):
    m, n = h.shape
    k_sizes = tuple(x.shape[1] for x in xs)
    k = sum(k_sizes)
    row = pl.BlockSpec((tm, n), lambda i: (i, 0))
    if last:
        out_shape = [jax.ShapeDtypeStruct((m, n), F32)]
    else:
        out_shape = [jax.ShapeDtypeStruct((m, n), F32), jax.ShapeDtypeStruct((m, n), BF16)]
    outs = pl.pallas_call(
        functools.partial(_out_proj_kernel, k_sizes=k_sizes, emit_h=not last),
        grid=(m // tm,),
        in_specs=[pl.BlockSpec((tm, ks), lambda i: (i, 0)) for ks in k_sizes]
        + [pl.BlockSpec((k, n), lambda i: (0, 0), pipeline_mode=pl.Buffered(1)), row,
           pl.BlockSpec((1, n), lambda i: (0, 0))],
        out_specs=[row] * len(out_shape),
        out_shape=out_shape,
        compiler_params=_params("arbitrary"),
        name="out_proj",
    )(*xs, w_bf16, h, norm_w.reshape(1, n))
    return outs


def _softplus(x):
    return jnp.maximum(x, 0.0) + jnp.log1p(jnp.exp(-jnp.abs(x)))


def _gates_kernel(x_ref, w_ref, alog_ref, dtb_ref, *rest, tm, norm_input):
    nh = DN_V_HEADS
    c = DN_CHUNK
    if norm_input:
        nw_ref, col_ref, row_ref, hn_ref, ws_ref = rest
        xf = x_ref[...]
        ms = jnp.mean(xf * xf, axis=-1, keepdims=True)
        x = (xf * lax.rsqrt(ms + EPS) * nw_ref[...]).astype(BF16)
        hn_ref[...] = x
    else:
        col_ref, row_ref, ws_ref = rest
        x = x_ref[...]

    @pl.when(pl.program_id(0) == 0)
    def _():
        ws_ref[...] = w_ref[...].astype(BF16)

    ri = lax.broadcasted_iota(jnp.int32, (c, c), 0)
    ci = lax.broadcasted_iota(jnp.int32, (c, c), 1)
    lower = jnp.where(ci <= ri, 1.0, 0.0).astype(BF16)

    p = lax.dot_general(x, ws_ref[...], _NT, preferred_element_type=F32)
    beta = jax.nn.sigmoid(p[:, :nh])
    g = -jnp.exp(alog_ref[...]) * _softplus(p + dtb_ref[...])
    hi = g.astype(BF16)
    r1 = g - hi.astype(F32)
    mid = r1.astype(BF16)
    lo = (r1 - mid.astype(F32)).astype(BF16)
    for ch in range(tm // c):
        rows = slice(ch * c, (ch + 1) * c)
        gc = sum(jnp.dot(lower, part[rows, :], preferred_element_type=F32) for part in (hi, mid, lo))
        for hq in range(DN_QK_HEADS):
            col_ref[hq, rows, 0:2] = beta[rows, 2 * hq:2 * hq + 2]
            col_ref[hq, rows, 2:4] = gc[:, nh + 2 * hq:nh + 2 * hq + 2]
        row_ref[ch] = gc.T[nh:2 * nh, :]


def _gates(x, wt, layer, a_log, dt_bias, batch, seq, norm_w=None, tm=512):
    m, k = x.shape
    nh = DN_V_HEADS
    nch = seq // DN_CHUNK
    assert GATE_OFF % LANES == 0
    steps = seq // tm
    norm_input = norm_w is not None

    def at_alpha_lanes(v):
        return jnp.zeros((1, LANES), F32).at[0, nh:2 * nh].set(v)

    in_specs = [pl.BlockSpec((tm, k), lambda i: (i, 0)),
                pl.BlockSpec((None, LANES, k), lambda i: (layer, GATE_OFF // LANES, 0)),
                pl.BlockSpec((1, LANES), lambda i: (0, 0)),
                pl.BlockSpec((1, LANES), lambda i: (0, 0))]
    out_specs = [pl.BlockSpec((None, DN_QK_HEADS, tm, 4), lambda i: (i // steps, 0, i % steps, 0)),
                 pl.BlockSpec((None, tm // DN_CHUNK, nh, DN_CHUNK), lambda i: (i // steps, i % steps, 0, 0))]
    out_shape = [jax.ShapeDtypeStruct((batch, DN_QK_HEADS, seq, 4), F32),
                 jax.ShapeDtypeStruct((batch, nch, nh, DN_CHUNK), F32)]
    args = [x, wt, at_alpha_lanes(a_log), at_alpha_lanes(dt_bias)]
    if norm_input:
        in_specs.append(pl.BlockSpec((1, k), lambda i: (0, 0)))
        out_specs.append(pl.BlockSpec((tm, k), lambda i: (i, 0)))
        out_shape.append(jax.ShapeDtypeStruct((m, k), BF16))
        args.append(norm_w.reshape(1, k))
    return pl.pallas_call(
        functools.partial(_gates_kernel, tm=tm, norm_input=norm_input),
        grid=(m // tm,),
        in_specs=in_specs,
        out_specs=out_specs,
        out_shape=out_shape,
        scratch_shapes=[pltpu.VMEM((LANES, k), BF16)],
        compiler_params=_params("arbitrary"),
        name="dn_gates",
    )(*args)


def _conv_silu(pad_ref, col0, ncols, cw_ref, t0, width):
    c = DN_CHUNK
    acc = None
    for i in range(width):
        parts = [pad_ref[col0 // HEAD_DIM + p, pl.ds(t0 + (8 - (width - 1) + i), c, stride=1), :]
                 for p in range(ncols // HEAD_DIM)]
        x = parts[0] if len(parts) == 1 else jnp.concatenate(parts, axis=1)
        term = cw_ref[i:i + 1, :] * x
        acc = term if acc is None else acc + term
    return _silu(acc)


def _l2norm(x):
    return x * lax.rsqrt(jnp.sum(x * x, axis=-1, keepdims=True) + EPS)


def _silu(x):
    return x / (1.0 + jnp.exp(-x))


def _unit_lower_inverse(mats):
    c = mats[0].shape[0]
    ri = lax.broadcasted_iota(jnp.int32, (c, c), 0)
    ci = lax.broadcasted_iota(jnp.int32, (c, c), 1)
    x = ri ^ ci
    zero = jnp.zeros((), BF16)
    eye = jnp.where(ri == ci, 1.0, 0.0).astype(BF16)
    ts = [eye - jnp.where(x < 2, a, zero) for a in mats]
    b = 2
    while b < c:
        level = (x >= b) & (x < 2 * b)
        ls = [jnp.where(level, a, zero) for a in mats]
        xs = [jnp.dot(t, l, preferred_element_type=F32).astype(BF16) for t, l in zip(ts, ls)]
        yield
        ts = [t - jnp.dot(xm, t, preferred_element_type=F32).astype(BF16) for t, xm in zip(ts, xs)]
        yield
        b *= 2
    return ts


def _deltanet_kernel(q_ref, k_ref, v_ref, z_ref, cwq_ref, cwk_ref, cwv_ref, gcol_ref, grow_ref, nw_ref,
                     o_ref, state_ref, gq_ref, cc_ref, op_ref, pad_ref, *, seq, unroll, scan_every):
    c = DN_CHUNK
    dk = HEAD_DIM
    nch = seq // c
    hq = pl.program_id(1)
    ri = lax.broadcasted_iota(jnp.int32, (c, c), 0)
    ci = lax.broadcasted_iota(jnp.int32, (c, c), 1)
    causal = ri >= ci
    strict = ri > ci

    pad_ref[:, 0:8, :] = jnp.zeros((4, 8, dk), F32)

    def pad_body(i, carry):
        r0 = pl.multiple_of(i * c, c)
        pad_ref[0, pl.ds(8 + r0, c), :] = q_ref[pl.ds(r0, c), :].astype(F32)
        pad_ref[1, pl.ds(8 + r0, c), :] = k_ref[pl.ds(r0, c), :].astype(F32)
        pad_ref[2, pl.ds(8 + r0, c), :] = v_ref[pl.ds(r0, c), 0:dk].astype(F32)
        pad_ref[3, pl.ds(8 + r0, c), :] = v_ref[pl.ds(r0, c), dk:2 * dk].astype(F32)
        return carry

    lax.fori_loop(0, nch, pad_body, 0)

    def prep_stages(i):
        chunks = [i * unroll + uu for uu in range(unroll)]
        t0s = [pl.multiple_of(n * c, c) for n in chunks]
        qs = [_l2norm(_conv_silu(pad_ref, 0, dk, cwq_ref, t0, DN_CONV)) * (dk ** -0.5) for t0 in t0s]
        ks = [_l2norm(_conv_silu(pad_ref, dk, dk, cwk_ref, t0, DN_CONV)) for t0 in t0s]
        v2s = [_conv_silu(pad_ref, 2 * dk, 2 * dk, cwv_ref, t0, DN_CONV) for t0 in t0s]
        kbs = [k.astype(BF16) for k in ks]
        kks = [lax.dot_general(kb, kb, _NT, preferred_element_type=F32) for kb in kbs]
        qks = [lax.dot_general(q.astype(BF16), kb, _NT, preferred_element_type=F32) for q, kb in zip(qs, kbs)]
        yield
        gcols = [gcol_ref[pl.ds(t0, c), :] for t0 in t0s]
        chains = [(u, j) for u in range(unroll) for j in range(2)]
        betas = [gcols[u][:, j:j + 1] for u, j in chains]
        gcs = [gcols[u][:, 2 + j:3 + j] for u, j in chains]
        grs = [grow_ref[chunks[u], pl.ds(2 * hq + j, 1), :] for u, j in chains]
        decays = [jnp.exp(jnp.where(causal, gc - gr, NEG_BIG)) for gc, gr in zip(gcs, grs)]
        mats = [jnp.where(strict, beta * kks[u] * decay, 0.0).astype(BF16)
                for (u, j), beta, decay in zip(chains, betas, decays)]
        ts = yield from _unit_lower_inverse(mats)
        egs = [jnp.exp(gc) for gc in gcs]
        rhss = [jnp.concatenate([v2s[u][:, j * dk:(j + 1) * dk] * beta, ks[u] * (beta * eg)], axis=1)
                for (u, j), beta, eg in zip(chains, betas, egs)]
        uws = [jnp.dot(t, rhs.astype(BF16), preferred_element_type=F32).astype(BF16)
               for t, rhs in zip(ts, rhss)]
        yield
        kds = [(ks[u] * jnp.exp(gc[c - 1:c, :] - gc)).astype(BF16) for (u, j), gc in zip(chains, gcs)]
        cgs = [lax.dot_general(kd, uw, _TN, preferred_element_type=F32) for kd, uw in zip(kds, uws)]
        yield
        oxs = [jnp.dot((qks[u] * decay).astype(BF16), uw, preferred_element_type=F32)
               for (u, j), decay, uw in zip(chains, decays, uws)]
        yield
        for (u, j), cg, ox, eg in zip(chains, cgs, oxs, egs):
            n = chunks[u]
            gq_ref[n, j, 0:c, :] = cg[:, dk:].astype(BF16)
            gq_ref[n, j, c:2 * c, :] = (qs[u] * eg - ox[:, dk:]).astype(BF16)
            cc_ref[n, j] = cg[:, :dk]
            op_ref[n, j] = ox[:, :dk]

    def scan_steps(i):
        for uu in range(unroll):
            n = i * unroll + uu
            t0 = pl.multiple_of(n * c, c)
            z2 = z_ref[pl.ds(t0, c), :].astype(F32)
            for j in range(2):
                s = state_ref[j]
                prod = jnp.dot(gq_ref[n, j], s.astype(BF16), preferred_element_type=F32)
                gr = grow_ref[n, pl.ds(2 * hq + j, 1), :]
                state_ref[j] = s * jnp.exp(gr[:, c - 1:c]) - prod[:c] + cc_ref[n, j]
                o = prod[c:] + op_ref[n, j]
                o = o * lax.rsqrt(jnp.mean(o * o, axis=-1, keepdims=True) + EPS) * nw_ref[...]
                z = z2[:, j * dk:(j + 1) * dk]
                o_ref[pl.ds(t0, c), j * dk:(j + 1) * dk] = (o * _silu(z)).astype(o_ref.dtype)
            yield

    def run_interleaved(prep, scan, every):
        for count, _ in enumerate(prep, start=1):
            if count % every == 0:
                next(scan, None)
        for _ in scan:
            pass

    state_ref[...] = jnp.zeros_like(state_ref)
    groups = nch // unroll
    for _ in prep_stages(0):
        pass

    def pipelined_body(i, carry):
        run_interleaved(prep_stages(i), scan_steps(i - 1), every=scan_every)
        return carry

    lax.fori_loop(1, groups, pipelined_body, 0)
    for _ in scan_steps(groups - 1):
        pass


def _deltanet(proj_a, conv_w_t, gcol, grow, norm_w, batch, seq, unroll=8, scan_every=1):
    dk = HEAD_DIM
    nq = DN_QK_HEADS
    c = DN_CHUNK
    nch = seq // c
    return pl.pallas_call(
        functools.partial(_deltanet_kernel, seq=seq, unroll=unroll, scan_every=scan_every),
        grid=(batch, nq),
        in_specs=[pl.BlockSpec((seq, dk), lambda b, h: (b, h)),
                  pl.BlockSpec((seq, dk), lambda b, h: (b, nq + h)),
                  pl.BlockSpec((seq, 2 * dk), lambda b, h: (b, nq + h)),
                  pl.BlockSpec((seq, 2 * dk), lambda b, h: (b, 2 * nq + h)),
                  pl.BlockSpec((DN_CONV, dk), lambda b, h: (0, h)),
                  pl.BlockSpec((DN_CONV, dk), lambda b, h: (0, nq + h)),
                  pl.BlockSpec((DN_CONV, 2 * dk), lambda b, h: (0, nq + h)),
                  pl.BlockSpec((None, None, seq, 4), lambda b, h: (b, h, 0, 0)),
                  pl.BlockSpec((None, nch, DN_V_HEADS, c), lambda b, h: (b, 0, 0, 0)),
                  pl.BlockSpec((1, dk), lambda b, h: (0, 0))],
        out_specs=pl.BlockSpec((seq, 2 * dk), lambda b, h: (b, h)),
        out_shape=jax.ShapeDtypeStruct((batch * seq, A_V), BF16),
        scratch_shapes=[pltpu.VMEM((2, dk, dk), F32),
                        pltpu.VMEM((nch, 2, 2 * c, dk), BF16),
                        pltpu.VMEM((nch, 2, dk, dk), F32),
                        pltpu.VMEM((nch, 2, c, dk), F32),
                        pltpu.VMEM((4, 8 + seq, dk), F32)],
        compiler_params=_params("arbitrary", "arbitrary"),
        name="deltanet",
    )(proj_a, proj_a, proj_a, proj_a, conv_w_t, conv_w_t, conv_w_t, gcol, grow, norm_w.reshape(1, dk))


def _rope_table_kernel(cos_ref, sin_ref):
    shape = cos_ref.shape
    pos = lax.broadcasted_iota(jnp.int32, shape, 0).astype(F32)
    lane = lax.broadcasted_iota(jnp.int32, shape, 1)
    idx = (lane % ROPE_HALF).astype(F32)
    inv = jnp.exp(idx * (-2.0 / ROPE_DIM) * jnp.log(jnp.float32(ROPE_THETA)))
    ang = pos * inv
    cos = jnp.cos(ang)
    sin = jnp.sin(ang)
    cos_ref[...] = jnp.where(lane < ROPE_DIM, cos, 1.0)
    sin_ref[...] = jnp.where(lane < ROPE_HALF, -sin, jnp.where(lane < ROPE_DIM, sin, 0.0))


def _rope_tables(seq):
    shp = jax.ShapeDtypeStruct((seq, HEAD_DIM), F32)
    return pl.pallas_call(_rope_table_kernel, out_shape=[shp, shp], name="rope_tables")()


def _attn_kernel(q0_ref, q1_ref, q2_ref, k_ref, v_ref, z_ref, cos_ref, sin_ref, o_ref,
                 qs_ref, ks_ref, vs_ref, m_ref, l_ref, acc_ref, bias_ref, *, seq, unroll):
    wb = SW_BLOCK
    dh = HEAD_DIM
    rows = 1024
    pr = lax.broadcasted_iota(jnp.int32, (dh, dh), 0)
    pc = lax.broadcasted_iota(jnp.int32, (dh, dh), 1)
    swap = jnp.where(((pc < ROPE_HALF) & (pr == pc + ROPE_HALF))
                     | ((pc >= ROPE_HALF) & (pc < ROPE_DIM) & (pr == pc - ROPE_HALF)), 1.0, 0.0).astype(BF16)

    def rope(x, cos, sin):
        swapped = jnp.dot(x.astype(BF16), swap, preferred_element_type=F32)
        return x * cos + swapped * sin

    def rope_body(i, carry):
        r0 = pl.multiple_of(i * rows, rows)
        cos = cos_ref[pl.ds(r0, rows), :]
        sin = sin_ref[pl.ds(r0, rows), :]
        for g, q_ref in enumerate((q0_ref, q1_ref, q2_ref)):
            qs_ref[g, pl.ds(r0, rows), :] = rope(q_ref[pl.ds(r0, rows), :].astype(F32), cos, sin) * (dh ** -0.5)
        ks_ref[pl.ds(r0, rows), :] = rope(k_ref[pl.ds(r0, rows), :].astype(F32), cos, sin)
        vs_ref[pl.ds(r0, rows), :] = v_ref[pl.ds(r0, rows), :].astype(F32)
        return carry

    lax.fori_loop(0, seq // rows, rope_body, 0)

    delta = lax.broadcasted_iota(jnp.int32, (wb, 2 * wb), 1) - lax.broadcasted_iota(jnp.int32, (wb, 2 * wb), 0)
    bias_ref[0] = jnp.where((delta >= 0) & (delta <= wb), 0.0, NEG_BIG)
    bias_ref[1] = jnp.where(delta <= 0, 0.0, NEG_BIG)

    order = sorted(range(len(SW_DILATIONS)), key=lambda gi: -SW_DILATIONS[gi])
    for pos_in_order, g in enumerate(order):
        dil = SW_DILATIONS[g]
        first = pos_in_order == 0
        last = pos_in_order == len(order) - 1
        nb = max(seq // dil // wb, 1)
        has_prev = nb > 1
        kw = 2 * wb if has_prev else wb
        ones = jnp.ones((kw, dh), BF16)

        def block_body(it, carry, g=g, dil=dil, nb=nb, has_prev=has_prev, kw=kw, ones=ones,
                       first=first, last=last):
            def rows_of(ref, s, count):
                if dil == 1:
                    return ref[pl.ds(s, count), :]
                return ref[pl.ds(s, count, stride=dil), :]

            idxs = [it * unroll + u for u in range(unroll)]
            ns = [idx & (nb - 1) for idx in idxs]
            starts = [lax.shift_right_logical(idx, nb.bit_length() - 1) + n * (dil * wb) for idx, n in zip(idxs, ns)]
            if dil == 1:
                starts = [pl.multiple_of(s, wb) for s in starts]
            if has_prev:
                kstarts = [jnp.where(n > 0, s - dil * wb, s) for s, n in zip(starts, ns)]
                biases = [bias_ref[jnp.where(n > 0, 0, 1)] for n in ns]
            else:
                kstarts = starts
                biases = [bias_ref[1, :, 0:wb] for _ in ns]
            qs = [rows_of(qs_ref.at[g], s, wb).astype(BF16) for s in starts]
            kwins = [rows_of(ks_ref, s, kw).astype(BF16) for s in kstarts]
            vaugs = [jnp.concatenate([rows_of(vs_ref, s, kw).astype(BF16), ones], axis=1) for s in kstarts]
            scs = [lax.dot_general(q, kwin, _NT, preferred_element_type=F32) + bias
                   for bias, q, kwin in zip(biases, qs, kwins)]
            m_blks = [jnp.max(sc, axis=-1, keepdims=True) for sc in scs]
            ps = [jnp.exp(sc - m).astype(BF16) for sc, m in zip(scs, m_blks)]
            pvs = [jnp.dot(p, vaug, preferred_element_type=F32) for p, vaug in zip(ps, vaugs)]
            for s, m_blk, pv in zip(starts, m_blks, pvs):
                m_blk = jnp.broadcast_to(m_blk, (wb, dh))
                acc_blk = pv[:, :dh]
                l_blk = pv[:, dh:]
                sl = pl.ds(s, wb) if dil == 1 else pl.ds(s, wb, stride=dil)
                if first:
                    m_ref[sl, :] = m_blk
                    l_ref[sl, :] = l_blk
                    acc_ref[sl, :] = acc_blk
                    continue
                m_old = m_ref[sl, :]
                m_new = jnp.maximum(m_old, m_blk)
                a_old = jnp.exp(m_old - m_new)
                a_blk = jnp.exp(m_blk - m_new)
                l_new = a_old * l_ref[sl, :] + a_blk * l_blk
                acc_new = a_old * acc_ref[sl, :] + a_blk * acc_blk
                if last:
                    z = z_ref[sl, :].astype(F32)
                    o_ref[sl, :] = (acc_new / l_new * _silu(z)).astype(o_ref.dtype)
                else:
                    m_ref[sl, :] = m_new
                    l_ref[sl, :] = l_new
                    acc_ref[sl, :] = acc_new
            return carry

        lax.fori_loop(0, dil * nb // unroll, block_body, 0)


def _dilated_attention(proj_b, cos_tab, sin_tab, batch, seq, unroll=16):
    dh = HEAD_DIM
    nh = SW_HEADS
    ng = len(SW_DILATIONS)

    def col(block):
        return pl.BlockSpec((seq, dh), lambda b, h: (b, block * nh + h))

    tab = pl.BlockSpec((seq, dh), lambda b, h: (0, 0))
    return pl.pallas_call(
        functools.partial(_attn_kernel, seq=seq, unroll=unroll),
        grid=(batch, nh),
        in_specs=[col(0), col(1), col(2), col(ng), col(ng + 1), col(ng + 2), tab, tab],
        out_specs=pl.BlockSpec((seq, dh), lambda b, h: (b, h)),
        out_shape=jax.ShapeDtypeStruct((batch * seq, nh * dh), BF16),
        scratch_shapes=[pltpu.VMEM((ng, seq, dh), F32), pltpu.VMEM((seq, dh), F32), pltpu.VMEM((seq, dh), F32),
                        pltpu.VMEM((seq, dh), F32), pltpu.VMEM((seq, dh), F32), pltpu.VMEM((seq, dh), F32),
                        pltpu.VMEM((2, SW_BLOCK, 2 * SW_BLOCK), F32)],
        compiler_params=_params("arbitrary", "arbitrary"),
        name="dilated_attn",
    )(proj_b, proj_b, proj_b, proj_b, proj_b, proj_b, cos_tab, sin_tab)


def _shortconv_kernel(x_ref, wb_ref, wc_ref, wu_ref, wz_ref, cw_ref, cast_src_ref, o_ref, cast_out_ref, wq_ref, halo_ref,
                      *, steps_per_seq):
    i = pl.program_id(1)
    cast_out_ref[...] = cast_src_ref[...].astype(BF16)

    @pl.when(i == 0)
    def _():
        for g, w_ref in enumerate((wb_ref, wc_ref, wu_ref, wz_ref)):
            wq_ref[g] = w_ref[...].astype(BF16)

    x = x_ref[...]
    tm = x.shape[0]
    gate_c = jnp.dot(x, wq_ref[1], preferred_element_type=F32)
    u = jnp.dot(x, wq_ref[2], preferred_element_type=F32)
    cu = gate_c * u
    prev = jnp.where(i % steps_per_seq != 0, halo_ref[...], 0.0)
    halo_ref[...] = cu[tm - 8:, :]
    xx = jnp.concatenate([prev, cu], axis=0)
    conv = cw_ref[SC_CONV - 1:SC_CONV, :] * cu
    for t in range(SC_CONV - 1):
        lo = 8 - (SC_CONV - 1) + t
        conv = conv + cw_ref[t:t + 1, :] * xx[lo:lo + tm, :]
    gate_b = jnp.dot(x, wq_ref[0], preferred_element_type=F32)
    z = jnp.dot(x, wq_ref[3], preferred_element_type=F32)
    o_ref[...] = (gate_b * conv * _silu(z)).astype(o_ref.dtype)


def _shortconv_mix(hn, w_in, layer, conv_w_t, w_out, seq, tm=1024, tn=256):
    m, k = hn.shape
    nblk = SC_WIDTH // tn
    n_i = m // tm
    _, r, c = w_out.shape
    rows = r // (nblk * n_i)
    assert rows * nblk * n_i == r

    def wspec(g):
        return pl.BlockSpec((None, k, tn), lambda j, i: (layer, 0, g * nblk + j))

    return pl.pallas_call(
        functools.partial(_shortconv_kernel, steps_per_seq=seq // tm),
        grid=(nblk, n_i),
        in_specs=[pl.BlockSpec((tm, k), lambda j, i: (i, 0)), wspec(0), wspec(1), wspec(2), wspec(3),
                  pl.BlockSpec((None, SC_CONV, tn), lambda j, i: (layer, 0, j)),
                  pl.BlockSpec((None, rows, c), lambda j, i: (layer, j * n_i + i, 0))],
        out_specs=[pl.BlockSpec((tm, tn), lambda j, i: (i, j)),
                   pl.BlockSpec((rows, c), lambda j, i: (j * n_i + i, 0))],
        out_shape=[jax.ShapeDtypeStruct((m, SC_WIDTH), BF16), jax.ShapeDtypeStruct((r, c), BF16)],
        scratch_shapes=[pltpu.VMEM((4, k, tn), BF16), pltpu.VMEM((8, tn), F32)],
        compiler_params=_params("arbitrary", "arbitrary"),
        name="shortconv_mix",
    )(hn, w_in, w_in, w_in, w_in, conv_w_t, w_out)


def kernel(x, norm_w, hyb_w_in, dn_conv_w, dn_a_log, dn_dt_bias, dn_norm_w, hyb_w_out, sc_w_in, sc_conv_w,
           sc_w_out, final_norm_w):
    batch, seq, d = x.shape
    h = x.reshape(batch * seq, d)
    cos_tab, sin_tab = _rope_tables(seq)
    dn_conv_t = jnp.swapaxes(dn_conv_w, 1, 2)
    sc_conv_t = jnp.swapaxes(sc_conv_w, 1, 2)
    hyb_wt = jnp.swapaxes(hyb_w_in, 1, 2)
    hn = None
    for layer in range(DEPTH):
        li = layer // 2
        last = layer == DEPTH - 1
        next_norm = final_norm_w if last else norm_w[layer + 1]
        if layer % 2 == 0:
            if hn is None:
                gcol, grow, hn = _gates(h, hyb_wt, li, dn_a_log[li], dn_dt_bias[li], batch, seq, norm_w=norm_w[layer])
            else:
                gcol, grow = _gates(hn, hyb_wt, li, dn_a_log[li], dn_dt_bias[li], batch, seq)
            proj_a, w_out = _matmul_nt(hn, hyb_wt, li, 0, A_WIDTH, BF16, cast_src=hyb_w_out)
            proj_b = _matmul_nt(hn, hyb_wt, li, B_OFF, B_WIDTH, BF16)
            y_a = _deltanet(proj_a, dn_conv_t[li], gcol, grow, dn_norm_w[li], batch, seq)
            y_b = _dilated_attention(proj_b, cos_tab, sin_tab, batch, seq)
            outs = _out_proj_residual_norm([y_a, y_b], w_out, h, next_norm, last)
        else:
            y, w_out = _shortconv_mix(hn, sc_w_in, li, sc_conv_t, sc_w_out, seq)
            outs = _out_proj_residual_norm([y], w_out, h, next_norm, last)
        if last:
            return outs[0].reshape(batch, seq, d)
        h, hn = outs
```

```python
import functools

import jax
import jax.numpy as jnp
from jax import lax
from jax.experimental import pallas as pl
from jax.experimental.pallas import tpu as pltpu

F32 = jnp.float32
BF16 = jnp.bfloat16

DEPTH = 4
HEAD_DIM = 128
EPS = 1e-6
DN_QK_HEADS = 8
DN_V_HEADS = 16
DN_CONV = 4
DN_CHUNK = 128
SW_HEADS = 8
SW_DILATIONS = (1, 4, 16)
SW_BLOCK = 128
ROPE_THETA = 500000.0
ROPE_DIM = HEAD_DIM // 4
ROPE_HALF = ROPE_DIM // 2
SC_WIDTH = 3072
SC_CONV = 3

A_Q = DN_QK_HEADS * HEAD_DIM
A_V = DN_V_HEADS * HEAD_DIM
A_WIDTH = 2 * A_Q + 2 * A_V
GATE_OFF = A_WIDTH
B_OFF = A_WIDTH + 2 * DN_V_HEADS
B_WIDTH = (len(SW_DILATIONS) + 3) * SW_HEADS * HEAD_DIM

LANES = 128
VMEM_LIMIT = 56 * 1024 * 1024
NEG_BIG = -1e30

_NT = (((1,), (1,)), ((), ()))
_TN = (((0,), (0,)), ((), ()))


def _params(*sem):
    return pltpu.CompilerParams(dimension_semantics=sem, vmem_limit_bytes=VMEM_LIMIT)


def _mm_nt_kernel(x_ref, w_ref, wx_ref, *rest, shift, with_cast):
    if with_cast:
        cast_src_ref, o_ref, cast_out_ref, wb_ref = rest
        cast_out_ref[...] = cast_src_ref[...].astype(BF16)
    else:
        o_ref, wb_ref = rest

    @pl.when(pl.program_id(1) == 0)
    def _():
        tn = wb_ref.shape[0]
        wb_ref[0:tn - shift, :] = w_ref[shift:tn, :].astype(BF16)
        if shift:
            wb_ref[tn - shift:tn, :] = wx_ref[...].astype(BF16)

    o_ref[...] = lax.dot_general(x_ref[...], wb_ref[...], _NT, preferred_element_type=F32).astype(o_ref.dtype)


def _matmul_nt(x, wt, layer, row0, n_rows, out_dtype, cast_src=None, tm=1024, tn=1024, shift_rows=32):
    m, k = x.shape
    shift = row0 % tn
    assert shift in (0, shift_rows) and tn % shift_rows == 0
    base = row0 // tn
    n_j, n_i = n_rows // tn, m // tm
    in_specs = [pl.BlockSpec((tm, k), lambda j, i: (i, 0)),
                pl.BlockSpec((None, tn, k), lambda j, i: (layer, base + j, 0)),
                pl.BlockSpec((None, shift_rows, k), lambda j, i: (layer, (base + j + 1) * (tn // shift_rows), 0))]
    out_specs = [pl.BlockSpec((tm, tn), lambda j, i: (i, j))]
    out_shape = [jax.ShapeDtypeStruct((m, n_rows), out_dtype)]
    args = [x, wt, wt]
    if cast_src is not None:
        _, r, c = cast_src.shape
        rows = r // (n_j * n_i)
        assert rows * n_j * n_i == r
        in_specs.append(pl.BlockSpec((None, rows, c), lambda j, i: (layer, j * n_i + i, 0)))
        out_specs.append(pl.BlockSpec((rows, c), lambda j, i: (j * n_i + i, 0)))
        out_shape.append(jax.ShapeDtypeStruct((r, c), BF16))
        args.append(cast_src)
    outs = pl.pallas_call(
        functools.partial(_mm_nt_kernel, shift=shift, with_cast=cast_src is not None),
        grid=(n_j, n_i),
        in_specs=in_specs,
        out_specs=out_specs,
        out_shape=out_shape,
        scratch_shapes=[pltpu.VMEM((tn, k), BF16)],
        compiler_params=_params("arbitrary", "arbitrary"),
        name="in_proj",
    )(*args)
    return outs if cast_src is not None else outs[0]


def _out_proj_kernel(*refs, k_sizes, emit_h):
    nx = len(k_sizes)
    x_refs = refs[:nx]
    w_ref, h_ref, nw_ref = refs[nx:nx + 3]
    out_refs = refs[nx + 3:]
    acc = h_ref[...]
    off = 0
    for x_ref, ks in zip(x_refs, k_sizes):
        acc = acc + jnp.dot(x_ref[...], w_ref[off:off + ks, :], preferred_element_type=F32)
        off += ks
    if emit_h:
        out_refs[0][...] = acc
    hn_ref = out_refs[-1]
    ms = jnp.mean(acc * acc, axis=-1, keepdims=True)
    hn_ref[...] = (acc * lax.rsqrt(ms + EPS) * nw_ref[...]).astype(hn_ref.dtype)


def _out_proj_residual_norm(xs, w_bf16, h, norm_w, last, tm=512):
    m, n = h.shape
    k_sizes = tuple(x.shape[1] for x in xs)
    k = sum(k_sizes)
    row = pl.BlockSpec((tm, n), lambda i: (i, 0))
    if last:
        out_shape = [jax.ShapeDtypeStruct((m, n), F32)]
    else:
        out_shape = [jax.ShapeDtypeStruct((m, n), F32), jax.ShapeDtypeStruct((m, n), BF16)]
    outs = pl.pallas_call(
        functools.partial(_out_proj_kernel, k_sizes=k_sizes, emit_h=not last),
        grid=(m // tm,),
        in_specs=[pl.BlockSpec((tm, ks), lambda i: (i, 0)) for ks in k_sizes]
        + [pl.BlockSpec((k, n), lambda i: (0, 0), pipeline_mode=pl.Buffered(1)), row,
           pl.BlockSpec((1, n), lambda i: (0, 0))],
        out_specs=[row] * len(out_shape),
        out_shape=out_shape,
        compiler_params=_params("arbitrary"),
        name="out_proj",
    )(*xs, w_bf16, h, norm_w.reshape(1, n))
    return outs


def _softplus(x):
    return jnp.maximum(x, 0.0) + jnp.log1p(jnp.exp(-jnp.abs(x)))


def _gates_kernel(x_ref, w_ref, alog_ref, dtb_ref, *rest, tm, norm_input):
    nh = DN_V_HEADS
    c = DN_CHUNK
    if norm_input:
        nw_ref, col_ref, row_ref, hn_ref, ws_ref = rest
        xf = x_ref[...]
        ms = jnp.mean(xf * xf, axis=-1, keepdims=True)
        x = (xf * lax.rsqrt(ms + EPS) * nw_ref[...]).astype(BF16)
        hn_ref[...] = x
    else:
        col_ref, row_ref, ws_ref = rest
        x = x_ref[...]

    @pl.when(pl.program_id(0) == 0)
    def _():
        ws_ref[...] = w_ref[...].astype(BF16)

    ri = lax.broadcasted_iota(jnp.int32, (c, c), 0)
    ci = lax.broadcasted_iota(jnp.int32, (c, c), 1)
    lower = jnp.where(ci <= ri, 1.0, 0.0).astype(BF16)

    p = lax.dot_general(x, ws_ref[...], _NT, preferred_element_type=F32)
    beta = jax.nn.sigmoid(p[:, :nh])
    g = -jnp.exp(alog_ref[...]) * _softplus(p + dtb_ref[...])
    hi = g.astype(BF16)
    r1 = g - hi.astype(F32)
    mid = r1.astype(BF16)
    lo = (r1 - mid.astype(F32)).astype(BF16)
    for ch in range(tm // c):
        rows = slice(ch * c, (ch + 1) * c)
        gc = sum(jnp.dot(lower, part[rows, :], preferred_element_type=F32) for part in (hi, mid, lo))
        for hq in range(DN_QK_HEADS):
            col_ref[hq, rows, 0:2] = beta[rows, 2 * hq:2 * hq + 2]
            col_ref[hq, rows, 2:4] = gc[:, nh + 2 * hq:nh + 2 * hq + 2]
        row_ref[ch] = gc.T[nh:2 * nh, :]


def _gates(x, wt, layer, a_log, dt_bias, batch, seq, norm_w=None, tm=1024):
    m, k = x.shape
    nh = DN_V_HEADS
    nch = seq // DN_CHUNK
    assert GATE_OFF % LANES == 0
    steps = seq // tm
    norm_input = norm_w is not None

    def at_alpha_lanes(v):
        return jnp.zeros((1, LANES), F32).at[0, nh:2 * nh].set(v)

    in_specs = [pl.BlockSpec((tm, k), lambda i: (i, 0)),
                pl.BlockSpec((None, LANES, k), lambda i: (layer, GATE_OFF // LANES, 0)),
                pl.BlockSpec((1, LANES), lambda i: (0, 0)),
                pl.BlockSpec((1, LANES), lambda i: (0, 0))]
    out_specs = [pl.BlockSpec((None, DN_QK_HEADS, tm, 4), lambda i: (i // steps, 0, i % steps, 0)),
                 pl.BlockSpec((None, tm // DN_CHUNK, nh, DN_CHUNK), lambda i: (i // steps, i % steps, 0, 0))]
    out_shape = [jax.ShapeDtypeStruct((batch, DN_QK_HEADS, seq, 4), F32),
                 jax.ShapeDtypeStruct((batch, nch, nh, DN_CHUNK), F32)]
    args = [x, wt, at_alpha_lanes(a_log), at_alpha_lanes(dt_bias)]
    if norm_input:
        in_specs.append(pl.BlockSpec((1, k), lambda i: (0, 0)))
        out_specs.append(pl.BlockSpec((tm, k), lambda i: (i, 0)))
        out_shape.append(jax.ShapeDtypeStruct((m, k), BF16))
        args.append(norm_w.reshape(1, k))
    return pl.pallas_call(
        functools.partial(_gates_kernel, tm=tm, norm_input=norm_input),
        grid=(m // tm,),
        in_specs=in_specs,
        out_specs=out_specs,
        out_shape=out_shape,
        scratch_shapes=[pltpu.VMEM((LANES, k), BF16)],
        compiler_params=_params("arbitrary"),
        name="dn_gates",
    )(*args)


def _conv_silu(pad_ref, col0, ncols, cw_ref, t0, width):
    c = DN_CHUNK
    acc = None
    for i in range(width):
        parts = [pad_ref[col0 // HEAD_DIM + p, pl.ds(t0 + (8 - (width - 1) + i), c, stride=1), :]
                 for p in range(ncols // HEAD_DIM)]
        x = parts[0] if len(parts) == 1 else jnp.concatenate(parts, axis=1)
        term = cw_ref[i:i + 1, :] * x
        acc = term if acc is None else acc + term
    return _silu(acc)


def _l2norm(x):
    return x * lax.rsqrt(jnp.sum(x * x, axis=-1, keepdims=True) + EPS)


def _silu(x):
    return x / (1.0 + jnp.exp(-x))


def _unit_lower_inverse(mats):
    c = mats[0].shape[0]
    ri = lax.broadcasted_iota(jnp.int32, (c, c), 0)
    ci = lax.broadcasted_iota(jnp.int32, (c, c), 1)
    x = ri ^ ci
    zero = jnp.zeros((), BF16)
    eye = jnp.where(ri == ci, 1.0, 0.0).astype(BF16)
    ts = [eye - jnp.where(x < 2, a, zero) for a in mats]
    b = 2
    while b < c:
        level = (x >= b) & (x < 2 * b)
        ls = [jnp.where(level, a, zero) for a in mats]
        xs = [jnp.dot(t, l, preferred_element_type=F32).astype(BF16) for t, l in zip(ts, ls)]
        yield
        ts = [t - jnp.dot(xm, t, preferred_element_type=F32).astype(BF16) for t, xm in zip(ts, xs)]
        yield
        b *= 2
    return ts


def _deltanet_kernel(q_ref, k_ref, v_ref, z_ref, cwq_ref, cwk_ref, cwv_ref, gcol_ref, grow_ref, nw_ref,
                     o_ref, state_ref, gq_ref, cc_ref, op_ref, pad_ref, *, seq, unroll, scan_every):
    c = DN_CHUNK
    dk = HEAD_DIM
    nch = seq // c
    hq = pl.program_id(1)
    ri = lax.broadcasted_iota(jnp.int32, (c, c), 0)
    ci = lax.broadcasted_iota(jnp.int32, (c, c), 1)
    causal = ri >= ci
    strict = ri > ci

    pad_ref[:, 0:8, :] = jnp.zeros((4, 8, dk), F32)

    def pad_body(i, carry):
        r0 = pl.multiple_of(i * c, c)
        pad_ref[0, pl.ds(8 + r0, c), :] = q_ref[pl.ds(r0, c), :].astype(F32)
        pad_ref[1, pl.ds(8 + r0, c), :] = k_ref[pl.ds(r0, c), :].astype(F32)
        pad_ref[2, pl.ds(8 + r0, c), :] = v_ref[pl.ds(r0, c), 0:dk].astype(F32)
        pad_ref[3, pl.ds(8 + r0, c), :] = v_ref[pl.ds(r0, c), dk:2 * dk].astype(F32)
        return carry

    lax.fori_loop(0, nch, pad_body, 0)

    def prep_stages(i):
        chunks = [i * unroll + uu for uu in range(unroll)]
        t0s = [pl.multiple_of(n * c, c) for n in chunks]
        qs = [_l2norm(_conv_silu(pad_ref, 0, dk, cwq_ref, t0, DN_CONV)) * (dk ** -0.5) for t0 in t0s]
        ks = [_l2norm(_conv_silu(pad_ref, dk, dk, cwk_ref, t0, DN_CONV)) for t0 in t0s]
        v2s = [_conv_silu(pad_ref, 2 * dk, 2 * dk, cwv_ref, t0, DN_CONV) for t0 in t0s]
        kbs = [k.astype(BF16) for k in ks]
        kks = [lax.dot_general(kb, kb, _NT, preferred_element_type=F32) for kb in kbs]
        qks = [lax.dot_general(q.astype(BF16), kb, _NT, preferred_element_type=F32) for q, kb in zip(qs, kbs)]
        yield
        gcols = [gcol_ref[pl.ds(t0, c), :] for t0 in t0s]
        chains = [(u, j) for u in range(unroll) for j in range(2)]
        betas = [gcols[u][:, j:j + 1] for u, j in chains]
        gcs = [gcols[u][:, 2 + j:3 + j] for u, j in chains]
        grs = [grow_ref[chunks[u], pl.ds(2 * hq + j, 1), :] for u, j in chains]
        decays = [jnp.exp(jnp.where(causal, gc - gr, NEG_BIG)) for gc, gr in zip(gcs, grs)]
        mats = [jnp.where(strict, beta * kks[u] * decay, 0.0).astype(BF16)
                for (u, j), beta, decay in zip(chains, betas, decays)]
        ts = yield from _unit_lower_inverse(mats)
        egs = [jnp.exp(gc) for gc in gcs]
        rhss = [jnp.concatenate([v2s[u][:, j * dk:(j + 1) * dk] * beta, ks[u] * (beta * eg)], axis=1)
                for (u, j), beta, eg in zip(chains, betas, egs)]
        uws = [jnp.dot(t, rhs.astype(BF16), preferred_element_type=F32).astype(BF16)
               for t, rhs in zip(ts, rhss)]
        yield
        kds = [(ks[u] * jnp.exp(gc[c - 1:c, :] - gc)).astype(BF16) for (u, j), gc in zip(chains, gcs)]
        cgs = [lax.dot_general(kd, uw, _TN, preferred_element_type=F32) for kd, uw in zip(kds, uws)]
        yield
        oxs = [jnp.dot((qks[u] * decay).astype(BF16), uw, preferred_element_type=F32)
               for (u, j), decay, uw in zip(chains, decays, uws)]
        yield
        for (u, j), cg, ox, eg in zip(chains, cgs, oxs, egs):
            n = chunks[u]
            gq_ref[n, j, 0:c, :] = cg[:, dk:].astype(BF16)
            gq_ref[n, j, c:2 * c, :] = (qs[u] * eg - ox[:, dk:]).astype(BF16)
            cc_ref[n, j] = cg[:, :dk]
            op_ref[n, j] = ox[:, :dk]

    def scan_steps(i):
        for uu in range(unroll):
            n = i * unroll + uu
            t0 = pl.multiple_of(n * c, c)
            z2 = z_ref[pl.ds(t0, c), :].astype(F32)
            for j in range(2):
                s = state_ref[j]
                prod = jnp.dot(gq_ref[n, j], s.astype(BF16), preferred_element_type=F32)
                gr = grow_ref[n, pl.ds(2 * hq + j, 1), :]
                state_ref[j] = s * jnp.exp(gr[:, c - 1:c]) - prod[:c] + cc_ref[n, j]
                o = prod[c:] + op_ref[n, j]
                o = o * lax.rsqrt(jnp.mean(o * o, axis=-1, keepdims=True) + EPS) * nw_ref[...]
                z = z2[:, j * dk:(j + 1) * dk]
                o_ref[pl.ds(t0, c), j * dk:(j + 1) * dk] = (o * _silu(z)).astype(o_ref.dtype)
            yield

    def run_interleaved(prep, scan, every):
        for count, _ in enumerate(prep, start=1):
            if count % every == 0:
                next(scan, None)
        for _ in scan:
            pass

    state_ref[...] = jnp.zeros_like(state_ref)
    groups = nch // unroll
    for _ in prep_stages(0):
        pass

    def pipelined_body(i, carry):
        run_interleaved(prep_stages(i), scan_steps(i - 1), every=scan_every)
        return carry

    lax.fori_loop(1, groups, pipelined_body, 0)
    for _ in scan_steps(groups - 1):
        pass


def _deltanet(proj_a, conv_w_t, gcol, grow, norm_w, batch, seq, unroll=8, scan_every=1):
    dk = HEAD_DIM
    nq = DN_QK_HEADS
    c = DN_CHUNK
    nch = seq // c
    return pl.pallas_call(
        functools.partial(_deltanet_kernel, seq=seq, unroll=unroll, scan_every=scan_every),
        grid=(batch, nq),
        in_specs=[pl.BlockSpec((seq, dk), lambda b, h: (b, h)),
                  pl.BlockSpec((seq, dk), lambda b, h: (b, nq + h)),
                  pl.BlockSpec((seq, 2 * dk), lambda b, h: (b, nq + h)),
                  pl.BlockSpec((seq, 2 * dk), lambda b, h: (b, 2 * nq + h)),
                  pl.BlockSpec((DN_CONV, dk), lambda b, h: (0, h)),
                  pl.BlockSpec((DN_CONV, dk), lambda b, h: (0, nq + h)),
                  pl.BlockSpec((DN_CONV, 2 * dk), lambda b, h: (0, nq + h)),
                  pl.BlockSpec((None, None, seq, 4), lambda b, h: (b, h, 0, 0)),
                  pl.BlockSpec((None, nch, DN_V_HEADS, c), lambda b, h: (b, 0, 0, 0)),
                  pl.BlockSpec((1, dk), lambda b, h: (0, 0))],
        out_specs=pl.BlockSpec((seq, 2 * dk), lambda b, h: (b, h)),
        out_shape=jax.ShapeDtypeStruct((batch * seq, A_V), BF16),
        scratch_shapes=[pltpu.VMEM((2, dk, dk), F32),
                        pltpu.VMEM((nch, 2, 2 * c, dk), BF16),
                        pltpu.VMEM((nch, 2, dk, dk), F32),
                        pltpu.VMEM((nch, 2, c, dk), F32),
                        pltpu.VMEM((4, 8 + seq, dk), F32)],
        compiler_params=_params("arbitrary", "arbitrary"),
        name="deltanet",
    )(proj_a, proj_a, proj_a, proj_a, conv_w_t, conv_w_t, conv_w_t, gcol, grow, norm_w.reshape(1, dk))


def _rope_table_kernel(cos_ref, sin_ref):
    shape = cos_ref.shape
    pos = lax.broadcasted_iota(jnp.int32, shape, 0).astype(F32)
    lane = lax.broadcasted_iota(jnp.int32, shape, 1)
    idx = (lane % ROPE_HALF).astype(F32)
    inv = jnp.exp(idx * (-2.0 / ROPE_DIM) * jnp.log(jnp.float32(ROPE_THETA)))
    ang = pos * inv
    cos = jnp.cos(ang)
    sin = jnp.sin(ang)
    cos_ref[...] = jnp.where(lane < ROPE_DIM, cos, 1.0)
    sin_ref[...] = jnp.where(lane < ROPE_HALF, -sin, jnp.where(lane < ROPE_DIM, sin, 0.0))


def _rope_tables(seq):
    shp = jax.ShapeDtypeStruct((seq, HEAD_DIM), F32)
    return pl.pallas_call(_rope_table_kernel, out_shape=[shp, shp], name="rope_tables")()


def _attn_kernel(q0_ref, q1_ref, q2_ref, k_ref, v_ref, z_ref, cos_ref, sin_ref, o_ref,
                 qs_ref, ks_ref, vs_ref, m_ref, l_ref, acc_ref, bias_ref, *, seq, unroll):
    wb = SW_BLOCK
    dh = HEAD_DIM
    rows = 1024
    pr = lax.broadcasted_iota(jnp.int32, (dh, dh), 0)
    pc = lax.broadcasted_iota(jnp.int32, (dh, dh), 1)
    swap = jnp.where(((pc < ROPE_HALF) & (pr == pc + ROPE_HALF))
                     | ((pc >= ROPE_HALF) & (pc < ROPE_DIM) & (pr == pc - ROPE_HALF)), 1.0, 0.0).astype(BF16)

    def rope(x, cos, sin):
        swapped = jnp.dot(x.astype(BF16), swap, preferred_element_type=F32)
        return x * cos + swapped * sin

    def rope_body(i, carry):
        r0 = pl.multiple_of(i * rows, rows)
        cos = cos_ref[pl.ds(r0, rows), :]
        sin = sin_ref[pl.ds(r0, rows), :]
        for g, q_ref in enumerate((q0_ref, q1_ref, q2_ref)):
            qs_ref[g, pl.ds(r0, rows), :] = rope(q_ref[pl.ds(r0, rows), :].astype(F32), cos, sin) * (dh ** -0.5)
        ks_ref[pl.ds(r0, rows), :] = rope(k_ref[pl.ds(r0, rows), :].astype(F32), cos, sin)
        vs_ref[pl.ds(r0, rows), :] = v_ref[pl.ds(r0, rows), :].astype(F32)
        return carry

    lax.fori_loop(0, seq // rows, rope_body, 0)

    delta = lax.broadcasted_iota(jnp.int32, (wb, 2 * wb), 1) - lax.broadcasted_iota(jnp.int32, (wb, 2 * wb), 0)
    bias_ref[0] = jnp.where((delta >= 0) & (delta <= wb), 0.0, NEG_BIG)
    bias_ref[1] = jnp.where(delta <= 0, 0.0, NEG_BIG)

    order = sorted(range(len(SW_DILATIONS)), key=lambda gi: -SW_DILATIONS[gi])
    for pos_in_order, g in enumerate(order):
        dil = SW_DILATIONS[g]
        first = pos_in_order == 0
        last = pos_in_order == len(order) - 1
        nb = max(seq // dil // wb, 1)
        has_prev = nb > 1
        kw = 2 * wb if has_prev else wb
        ones = jnp.ones((kw, dh), BF16)

        def block_body(it, carry, g=g, dil=dil, nb=nb, has_prev=has_prev, kw=kw, ones=ones,
                       first=first, last=last):
            def rows_of(ref, s, count):
                if dil == 1:
                    return ref[pl.ds(s, count), :]
                return ref[pl.ds(s, count, stride=dil), :]

            idxs = [it * unroll + u for u in range(unroll)]
            ns = [idx & (nb - 1) for idx in idxs]
            starts = [lax.shift_right_logical(idx, nb.bit_length() - 1) + n * (dil * wb) for idx, n in zip(idxs, ns)]
            if dil == 1:
                starts = [pl.multiple_of(s, wb) for s in starts]
            if has_prev:
                kstarts = [jnp.where(n > 0, s - dil * wb, s) for s, n in zip(starts, ns)]
                biases = [bias_ref[jnp.where(n > 0, 0, 1)] for n in ns]
            else:
                kstarts = starts
                biases = [bias_ref[1, :, 0:wb] for _ in ns]
            qs = [rows_of(qs_ref.at[g], s, wb).astype(BF16) for s in starts]
            kwins = [rows_of(ks_ref, s, kw).astype(BF16) for s in kstarts]
            vaugs = [jnp.concatenate([rows_of(vs_ref, s, kw).astype(BF16), ones], axis=1) for s in kstarts]
            scs = [lax.dot_general(q, kwin, _NT, preferred_element_type=F32) + bias
                   for bias, q, kwin in zip(biases, qs, kwins)]
            m_blks = [jnp.max(sc, axis=-1, keepdims=True) for sc in scs]
            ps = [jnp.exp(sc - m).astype(BF16) for sc, m in zip(scs, m_blks)]
            pvs = [jnp.dot(p, vaug, preferred_element_type=F32) for p, vaug in zip(ps, vaugs)]
            for s, m_blk, pv in zip(starts, m_blks, pvs):
                m_blk = jnp.broadcast_to(m_blk, (wb, dh))
                acc_blk = pv[:, :dh]
                l_blk = pv[:, dh:]
                sl = pl.ds(s, wb) if dil == 1 else pl.ds(s, wb, stride=dil)
                if first:
                    m_ref[sl, :] = m_blk
                    l_ref[sl, :] = l_blk
                    acc_ref[sl, :] = acc_blk
                    continue
                m_old = m_ref[sl, :]
                m_new = jnp.maximum(m_old, m_blk)
                a_old = jnp.exp(m_old - m_new)
                a_blk = jnp.exp(m_blk - m_new)
                l_new = a_old * l_ref[sl, :] + a_blk * l_blk
                acc_new = a_old * acc_ref[sl, :] + a_blk * acc_blk
                if last:
                    z = z_ref[sl, :].astype(F32)
                    o_ref[sl, :] = (acc_new / l_new * _silu(z)).astype(o_ref.dtype)
                else:
                    m_ref[sl, :] = m_new
                    l_ref[sl, :] = l_new
                    acc_ref[sl, :] = acc_new
            return carry

        lax.fori_loop(0, dil * nb // unroll, block_body, 0)


def _dilated_attention(proj_b, cos_tab, sin_tab, batch, seq, unroll=16):
    dh = HEAD_DIM
    nh = SW_HEADS
    ng = len(SW_DILATIONS)

    def col(block):
        return pl.BlockSpec((seq, dh), lambda b, h: (b, block * nh + h))

    tab = pl.BlockSpec((seq, dh), lambda b, h: (0, 0))
    return pl.pallas_call(
        functools.partial(_attn_kernel, seq=seq, unroll=unroll),
        grid=(batch, nh),
        in_specs=[col(0), col(1), col(2), col(ng), col(ng + 1), col(ng + 2), tab, tab],
        out_specs=pl.BlockSpec((seq, dh), lambda b, h: (b, h)),
        out_shape=jax.ShapeDtypeStruct((batch * seq, nh * dh), BF16),
        scratch_shapes=[pltpu.VMEM((ng, seq, dh), F32), pltpu.VMEM((seq, dh), F32), pltpu.VMEM((seq, dh), F32),
                        pltpu.VMEM((seq, dh), F32), pltpu.VMEM((seq, dh), F32), pltpu.VMEM((seq, dh), F32),
                        pltpu.VMEM((2, SW_BLOCK, 2 * SW_BLOCK), F32)],
        compiler_params=_params("arbitrary", "arbitrary"),
        name="dilated_attn",
    )(proj_b, proj_b, proj_b, proj_b, proj_b, proj_b, cos_tab, sin_tab)


def _shortconv_kernel(x_ref, wb_ref, wc_ref, wu_ref, wz_ref, cw_ref, cast_src_ref, o_ref, cast_out_ref, wq_ref, halo_ref,
                      *, steps_per_seq):
    i = pl.program_id(1)
    cast_out_ref[...] = cast_src_ref[...].astype(BF16)

    @pl.when(i == 0)
    def _():
        for g, w_ref in enumerate((wb_ref, wc_ref, wu_ref, wz_ref)):
            wq_ref[g] = w_ref[...].astype(BF16)

    x = x_ref[...]
    tm = x.shape[0]
    gate_c = jnp.dot(x, wq_ref[1], preferred_element_type=F32)
    u = jnp.dot(x, wq_ref[2], preferred_element_type=F32)
    cu = gate_c * u
    prev = jnp.where(i % steps_per_seq != 0, halo_ref[...], 0.0)
    halo_ref[...] = cu[tm - 8:, :]
    xx = jnp.concatenate([prev, cu], axis=0)
    conv = cw_ref[SC_CONV - 1:SC_CONV, :] * cu
    for t in range(SC_CONV - 1):
        lo = 8 - (SC_CONV - 1) + t
        conv = conv + cw_ref[t:t + 1, :] * xx[lo:lo + tm, :]
    gate_b = jnp.dot(x, wq_ref[0], preferred_element_type=F32)
    z = jnp.dot(x, wq_ref[3], preferred_element_type=F32)
    o_ref[...] = (gate_b * conv * _silu(z)).astype(o_ref.dtype)


def _shortconv_mix(hn, w_in, layer, conv_w_t, w_out, seq, tm=1024, tn=256):
    m, k = hn.shape
    nblk = SC_WIDTH // tn
    n_i = m // tm
    _, r, c = w_out.shape
    rows = r // (nblk * n_i)
    assert rows * nblk * n_i == r

    def wspec(g):
        return pl.BlockSpec((None, k, tn), lambda j, i: (layer, 0, g * nblk + j))

    return pl.pallas_call(
        functools.partial(_shortconv_kernel, steps_per_seq=seq // tm),
        grid=(nblk, n_i),
        in_specs=[pl.BlockSpec((tm, k), lambda j, i: (i, 0)), wspec(0), wspec(1), wspec(2), wspec(3),
                  pl.BlockSpec((None, SC_CONV, tn), lambda j, i: (layer, 0, j)),
                  pl.BlockSpec((None, rows, c), lambda j, i: (layer, j * n_i + i, 0))],
        out_specs=[pl.BlockSpec((tm, tn), lambda j, i: (i, j)),
                   pl.BlockSpec((rows, c), lambda j, i: (j * n_i + i, 0))],
        out_shape=[jax.ShapeDtypeStruct((m, SC_WIDTH), BF16), jax.ShapeDtypeStruct((r, c), BF16)],
        scratch_shapes=[pltpu.VMEM((4, k, tn), BF16), pltpu.VMEM((8, tn), F32)],
        compiler_params=_params("arbitrary", "arbitrary"),
        name="shortconv_mix",
    )(hn, w_in, w_in, w_in, w_in, conv_w_t, w_out)


def kernel(x, norm_w, hyb_w_in, dn_conv_w, dn_a_log, dn_dt_bias, dn_norm_w, hyb_w_out, sc_w_in, sc_conv_w,
           sc_w_out, final_norm_w):
    batch, seq, d = x.shape
    h = x.reshape(batch * seq, d)
    cos_tab, sin_tab = _rope_tables(seq)
    dn_conv_t = jnp.swapaxes(dn_conv_w, 1, 2)
    sc_conv_t = jnp.swapaxes(sc_conv_w, 1, 2)
    hyb_wt = jnp.swapaxes(hyb_w_in, 1, 2)
    hn = None
    for layer in range(DEPTH):
        li = layer // 2
        last = layer == DEPTH - 1
        next_norm = final_norm_w if last else norm_w[layer + 1]
        if layer % 2 == 0:
            if hn is None:
                gcol, grow, hn = _gates(h, hyb_wt, li, dn_a_log[li], dn_dt_bias[li], batch, seq, norm_w=norm_w[layer])
            else:
                gcol, grow = _gates(hn, hyb_wt, li, dn_a_log[li], dn_dt_bias[li], batch, seq)
            proj_a, w_out = _matmul_nt(hn, hyb_wt, li, 0, A_WIDTH, BF16, cast_src=hyb_w_out)
            proj_b = _matmul_nt(hn, hyb_wt, li, B_OFF, B_WIDTH, BF16)
            y_a = _deltanet(proj_a, dn_conv_t[li], gcol, grow, dn_norm_w[li], batch, seq)
            y_b = _dilated_attention(proj_b, cos_tab, sin_tab, batch, seq)
            outs = _out_proj_residual_norm([y_a, y_b], w_out, h, next_norm, last)
        else:
            y, w_out = _shortconv_mix(hn, sc_w_in, li, sc_conv_t, sc_w_out, seq)
            outs = _out_proj_residual_norm([y], w_out, h, next_norm, last)
        if last:
            return outs[0].reshape(batch, seq, d)
        h, hn = outs
```

```python
import functools

import jax
import jax.numpy as jnp
from jax import lax
from jax.experimental import pallas as pl
from jax.experimental.pallas import tpu as pltpu

F32 = jnp.float32
BF16 = jnp.bfloat16

DEPTH = 4
HEAD_DIM = 128
EPS = 1e-6
DN_QK_HEADS = 8
DN_V_HEADS = 16
DN_CONV = 4
DN_CHUNK = 128
SW_HEADS = 8
SW_DILATIONS = (1, 4, 16)
SW_BLOCK = 128
ROPE_THETA = 500000.0
ROPE_DIM = HEAD_DIM // 4
ROPE_HALF = ROPE_DIM // 2
SC_WIDTH = 3072
SC_CONV = 3

A_Q = DN_QK_HEADS * HEAD_DIM
A_V = DN_V_HEADS * HEAD_DIM
A_WIDTH = 2 * A_Q + 2 * A_V
GATE_OFF = A_WIDTH
B_OFF = A_WIDTH + 2 * DN_V_HEADS
B_WIDTH = (len(SW_DILATIONS) + 3) * SW_HEADS * HEAD_DIM

LANES = 128
VMEM_LIMIT = 56 * 1024 * 1024
NEG_BIG = -1e30

_NT = (((1,), (1,)), ((), ()))
_TN = (((0,), (0,)), ((), ()))


def _params(*sem):
    return pltpu.CompilerParams(dimension_semantics=sem, vmem_limit_bytes=VMEM_LIMIT)


def _mm_nt_kernel(x_ref, w_ref, wx_ref, *rest, shift, shift_from, with_cast):
    if with_cast:
        cast_src_ref, o_ref, cast_out_ref, wb_ref = rest
        cast_out_ref[...] = cast_src_ref[...].astype(BF16)
    else:
        o_ref, wb_ref = rest
    j = pl.program_id(0)
    first_row_tile = pl.program_id(1) == 0
    tn = wb_ref.shape[0]

    @pl.when(first_row_tile & (j < shift_from))
    def _():
        wb_ref[...] = w_ref[...].astype(BF16)

    @pl.when(first_row_tile & (j >= shift_from))
    def _():
        wb_ref[0:tn - shift, :] = w_ref[shift:tn, :].astype(BF16)
        wb_ref[tn - shift:tn, :] = wx_ref[...].astype(BF16)

    o_ref[...] = lax.dot_general(x_ref[...], wb_ref[...], _NT, preferred_element_type=F32).astype(o_ref.dtype)


def _matmul_nt(x, wt, layer, n_rows, skip_at, skip, out_dtype, cast_src=None, tm=1024, tn=1024):
    m, k = x.shape
    assert skip_at % tn == 0 and tn % skip == 0
    shift, shift_rows, base = skip, skip, 0
    n_j, n_i = n_rows // tn, m // tm
    in_specs = [pl.BlockSpec((tm, k), lambda j, i: (i, 0)),
                pl.BlockSpec((None, tn, k), lambda j, i: (layer, base + j, 0)),
                pl.BlockSpec((None, shift_rows, k), lambda j, i: (layer, (base + j + 1) * (tn // shift_rows), 0))]
    out_specs = [pl.BlockSpec((tm, tn), lambda j, i: (i, j))]
    out_shape = [jax.ShapeDtypeStruct((m, n_rows), out_dtype)]
    args = [x, wt, wt]
    if cast_src is not None:
        _, r, c = cast_src.shape
        rows = r // (n_j * n_i)
        assert rows * n_j * n_i == r
        in_specs.append(pl.BlockSpec((None, rows, c), lambda j, i: (layer, j * n_i + i, 0)))
        out_specs.append(pl.BlockSpec((rows, c), lambda j, i: (j * n_i + i, 0)))
        out_shape.append(jax.ShapeDtypeStruct((r, c), BF16))
        args.append(cast_src)
    outs = pl.pallas_call(
        functools.partial(_mm_nt_kernel, shift=shift, shift_from=skip_at // tn, with_cast=cast_src is not None),
        grid=(n_j, n_i),
        in_specs=in_specs,
        out_specs=out_specs,
        out_shape=out_shape,
        scratch_shapes=[pltpu.VMEM((tn, k), BF16)],
        compiler_params=_params("arbitrary", "arbitrary"),
        name="in_proj",
    )(*args)
    return outs if cast_src is not None else outs[0]


def _out_proj_kernel(*refs, k_sizes, emit_h):
    nx = len(k_sizes)
    x_refs = refs[:nx]
    w_ref, h_ref, nw_ref = refs[nx:nx + 3]
    out_refs = refs[nx + 3:]
    acc = h_ref[...]
    off = 0
    for x_ref, ks in zip(x_refs, k_sizes):
        acc = acc + jnp.dot(x_ref[...], w_ref[off:off + ks, :], preferred_element_type=F32)
        off += ks
    if emit_h:
        out_refs[0][...] = acc
    hn_ref = out_refs[-1]
    ms = jnp.mean(acc * acc, axis=-1, keepdims=True)
    hn_ref[...] = (acc * lax.rsqrt(ms + EPS) * nw_ref[...]).astype(hn_ref.dtype)


def _out_proj_residual_norm(xs, w_bf16, h, norm_w, last, tm=512):
    m, n = h.shape
    k_sizes = tuple(x.shape[1] for x in xs)
    k = sum(k_sizes)
    row = pl.BlockSpec((tm, n), lambda i: (i, 0))
    if last:
        out_shape = [jax.ShapeDtypeStruct((m, n), F32)]
    else:
        out_shape = [jax.ShapeDtypeStruct((m, n), F32), jax.ShapeDtypeStruct((m, n), BF16)]
    outs = pl.pallas_call(
        functools.partial(_out_proj_kernel, k_sizes=k_sizes, emit_h=not last),
        grid=(m // tm,),
        in_specs=[pl.BlockSpec((tm, ks), lambda i: (i, 0)) for ks in k_sizes]
        + [pl.BlockSpec((k, n), lambda i: (0, 0), pipeline_mode=pl.Buffered(1)), row,
           pl.BlockSpec((1, n), lambda i: (0, 0))],
        out_specs=[row] * len(out_shape),
        out_shape=out_shape,
        compiler_params=_params("arbitrary"),
        name="out_proj",
    )(*xs, w_bf16, h, norm_w.reshape(1, n))
    return outs


def _softplus(x):
    return jnp.maximum(x, 0.0) + jnp.log1p(jnp.exp(-jnp.abs(x)))


def _gates_kernel(x_ref, w_ref, alog_ref, dtb_ref, *rest, tm, norm_input):
    nh = DN_V_HEADS
    c = DN_CHUNK
    if norm_input:
        nw_ref, col_ref, row_ref, hn_ref, ws_ref = rest
        xf = x_ref[...]
        ms = jnp.mean(xf * xf, axis=-1, keepdims=True)
        x = (xf * lax.rsqrt(ms + EPS) * nw_ref[...]).astype(BF16)
        hn_ref[...] = x
    else:
        col_ref, row_ref, ws_ref = rest
        x = x_ref[...]

    @pl.when(pl.program_id(0) == 0)
    def _():
        ws_ref[...] = w_ref[...].astype(BF16)

    ri = lax.broadcasted_iota(jnp.int32, (c, c), 0)
    ci = lax.broadcasted_iota(jnp.int32, (c, c), 1)
    lower = jnp.where(ci <= ri, 1.0, 0.0).astype(BF16)

    p = lax.dot_general(x, ws_ref[...], _NT, preferred_element_type=F32)
    beta = jax.nn.sigmoid(p[:, :nh])
    g = -jnp.exp(alog_ref[...]) * _softplus(p + dtb_ref[...])
    hi = g.astype(BF16)
    r1 = g - hi.astype(F32)
    mid = r1.astype(BF16)
    lo = (r1 - mid.astype(F32)).astype(BF16)
    for ch in range(tm // c):
        rows = slice(ch * c, (ch + 1) * c)
        gc = sum(jnp.dot(lower, part[rows, :], preferred_element_type=F32) for part in (hi, mid, lo))
        for hq in range(DN_QK_HEADS):
            col_ref[hq, rows, 0:2] = beta[rows, 2 * hq:2 * hq + 2]
            col_ref[hq, rows, 2:4] = gc[:, nh + 2 * hq:nh + 2 * hq + 2]
        row_ref[ch] = gc.T[nh:2 * nh, :]


def _gates(x, wt, layer, a_log, dt_bias, batch, seq, norm_w=None, tm=1024):
    m, k = x.shape
    nh = DN_V_HEADS
    nch = seq // DN_CHUNK
    assert GATE_OFF % LANES == 0
    steps = seq // tm
    norm_input = norm_w is not None

    def at_alpha_lanes(v):
        return jnp.zeros((1, LANES), F32).at[0, nh:2 * nh].set(v)

    in_specs = [pl.BlockSpec((tm, k), lambda i: (i, 0)),
                pl.BlockSpec((None, LANES, k), lambda i: (layer, GATE_OFF // LANES, 0)),
                pl.BlockSpec((1, LANES), lambda i: (0, 0)),
                pl.BlockSpec((1, LANES), lambda i: (0, 0))]
    out_specs = [pl.BlockSpec((None, DN_QK_HEADS, tm, 4), lambda i: (i // steps, 0, i % steps, 0)),
                 pl.BlockSpec((None, tm // DN_CHUNK, nh, DN_CHUNK), lambda i: (i // steps, i % steps, 0, 0))]
    out_shape = [jax.ShapeDtypeStruct((batch, DN_QK_HEADS, seq, 4), F32),
                 jax.ShapeDtypeStruct((batch, nch, nh, DN_CHUNK), F32)]
    args = [x, wt, at_alpha_lanes(a_log), at_alpha_lanes(dt_bias)]
    if norm_input:
        in_specs.append(pl.BlockSpec((1, k), lambda i: (0, 0)))
        out_specs.append(pl.BlockSpec((tm, k), lambda i: (i, 0)))
        out_shape.append(jax.ShapeDtypeStruct((m, k), BF16))
        args.append(norm_w.reshape(1, k))
    return pl.pallas_call(
        functools.partial(_gates_kernel, tm=tm, norm_input=norm_input),
        grid=(m // tm,),
        in_specs=in_specs,
        out_specs=out_specs,
        out_shape=out_shape,
        scratch_shapes=[pltpu.VMEM((LANES, k), BF16)],
        compiler_params=_params("arbitrary"),
        name="dn_gates",
    )(*args)


def _conv_silu(pad_ref, col0, ncols, cw_ref, t0, width):
    c = DN_CHUNK
    acc = None
    for i in range(width):
        parts = [pad_ref[col0 // HEAD_DIM + p, pl.ds(t0 + (8 - (width - 1) + i), c, stride=1), :]
                 for p in range(ncols // HEAD_DIM)]
        x = parts[0] if len(parts) == 1 else jnp.concatenate(parts, axis=1)
        term = cw_ref[i:i + 1, :] * x
        acc = term if acc is None else acc + term
    return _silu(acc)


def _l2norm(x):
    return x * lax.rsqrt(jnp.sum(x * x, axis=-1, keepdims=True) + EPS)


def _silu(x):
    return x / (1.0 + jnp.exp(-x))


def _unit_lower_inverse(mats):
    c = mats[0].shape[0]
    ri = lax.broadcasted_iota(jnp.int32, (c, c), 0)
    ci = lax.broadcasted_iota(jnp.int32, (c, c), 1)
    x = ri ^ ci
    zero = jnp.zeros((), BF16)
    eye = jnp.where(ri == ci, 1.0, 0.0).astype(BF16)
    ts = [eye - jnp.where(x < 2, a, zero) for a in mats]
    b = 2
    while b < c:
        level = (x >= b) & (x < 2 * b)
        ls = [jnp.where(level, a, zero) for a in mats]
        xs = [jnp.dot(t, l, preferred_element_type=F32).astype(BF16) for t, l in zip(ts, ls)]
        yield
        ts = [t - jnp.dot(xm, t, preferred_element_type=F32).astype(BF16) for t, xm in zip(ts, xs)]
        yield
        b *= 2
    return ts


def _deltanet_kernel(q_ref, k_ref, v_ref, z_ref, cwq_ref, cwk_ref, cwv_ref, gcol_ref, grow_ref, nw_ref,
                     o_ref, state_ref, gq_ref, cc_ref, op_ref, pad_ref, *, seq, unroll, scan_every):
    c = DN_CHUNK
    dk = HEAD_DIM
    nch = seq // c
    hq = pl.program_id(1)
    ri = lax.broadcasted_iota(jnp.int32, (c, c), 0)
    ci = lax.broadcasted_iota(jnp.int32, (c, c), 1)
    causal = ri >= ci
    strict = ri > ci

    pad_ref[:, 0:8, :] = jnp.zeros((4, 8, dk), F32)

    def pad_body(i, carry):
        r0 = pl.multiple_of(i * c, c)
        pad_ref[0, pl.ds(8 + r0, c), :] = q_ref[pl.ds(r0, c), :].astype(F32)
        pad_ref[1, pl.ds(8 + r0, c), :] = k_ref[pl.ds(r0, c), :].astype(F32)
        pad_ref[2, pl.ds(8 + r0, c), :] = v_ref[pl.ds(r0, c), 0:dk].astype(F32)
        pad_ref[3, pl.ds(8 + r0, c), :] = v_ref[pl.ds(r0, c), dk:2 * dk].astype(F32)
        return carry

    lax.fori_loop(0, nch, pad_body, 0)

    def prep_stages(i):
        chunks = [i * unroll + uu for uu in range(unroll)]
        t0s = [pl.multiple_of(n * c, c) for n in chunks]
        qs = [_l2norm(_conv_silu(pad_ref, 0, dk, cwq_ref, t0, DN_CONV)) * (dk ** -0.5) for t0 in t0s]
        ks = [_l2norm(_conv_silu(pad_ref, dk, dk, cwk_ref, t0, DN_CONV)) for t0 in t0s]
        v2s = [_conv_silu(pad_ref, 2 * dk, 2 * dk, cwv_ref, t0, DN_CONV) for t0 in t0s]
        kbs = [k.astype(BF16) for k in ks]
        kks = [lax.dot_general(kb, kb, _NT, preferred_element_type=F32) for kb in kbs]
        qks = [lax.dot_general(q.astype(BF16), kb, _NT, preferred_element_type=F32) for q, kb in zip(qs, kbs)]
        yield
        gcols = [gcol_ref[pl.ds(t0, c), :] for t0 in t0s]
        chains = [(u, j) for u in range(unroll) for j in range(2)]
        betas = [gcols[u][:, j:j + 1] for u, j in chains]
        gcs = [gcols[u][:, 2 + j:3 + j] for u, j in chains]
        grs = [grow_ref[chunks[u], pl.ds(2 * hq + j, 1), :] for u, j in chains]
        decays = [jnp.exp(jnp.where(causal, gc - gr, NEG_BIG)) for gc, gr in zip(gcs, grs)]
        mats = [jnp.where(strict, beta * kks[u] * decay, 0.0).astype(BF16)
                for (u, j), beta, decay in zip(chains, betas, decays)]
        ts = yield from _unit_lower_inverse(mats)
        egs = [jnp.exp(gc) for gc in gcs]
        rhss = [jnp.concatenate([v2s[u][:, j * dk:(j + 1) * dk] * beta, ks[u] * (beta * eg)], axis=1)
                for (u, j), beta, eg in zip(chains, betas, egs)]
        uws = [jnp.dot(t, rhs.astype(BF16), preferred_element_type=F32).astype(BF16)
               for t, rhs in zip(ts, rhss)]
        yield
        kds = [(ks[u] * jnp.exp(gc[c - 1:c, :] - gc)).astype(BF16) for (u, j), gc in zip(chains, gcs)]
        cgs = [lax.dot_general(kd, uw, _TN, preferred_element_type=F32) for kd, uw in zip(kds, uws)]
        yield
        oxs = [jnp.dot((qks[u] * decay).astype(BF16), uw, preferred_element_type=F32)
               for (u, j), decay, uw in zip(chains, decays, uws)]
        yield
        for (u, j), cg, ox, eg in zip(chains, cgs, oxs, egs):
            n = chunks[u]
            gq_ref[n, j, 0:c, :] = cg[:, dk:].astype(BF16)
            gq_ref[n, j, c:2 * c, :] = (qs[u] * eg - ox[:, dk:]).astype(BF16)
            cc_ref[n, j] = cg[:, :dk]
            op_ref[n, j] = ox[:, :dk]

    def scan_steps(i):
        for uu in range(unroll):
            n = i * unroll + uu
            t0 = pl.multiple_of(n * c, c)
            z2 = z_ref[pl.ds(t0, c), :].astype(F32)
            for j in range(2):
                s = state_ref[j]
                prod = jnp.dot(gq_ref[n, j], s.astype(BF16), preferred_element_type=F32)
                gr = grow_ref[n, pl.ds(2 * hq + j, 1), :]
                state_ref[j] = s * jnp.exp(gr[:, c - 1:c]) - prod[:c] + cc_ref[n, j]
                o = prod[c:] + op_ref[n, j]
                o = o * lax.rsqrt(jnp.mean(o * o, axis=-1, keepdims=True) + EPS) * nw_ref[...]
                z = z2[:, j * dk:(j + 1) * dk]
                o_ref[pl.ds(t0, c), j * dk:(j + 1) * dk] = (o * _silu(z)).astype(o_ref.dtype)
            yield

    def run_interleaved(prep, scan, every):
        for count, _ in enumerate(prep, start=1):
            if count % every == 0:
                next(scan, None)
        for _ in scan:
            pass

    state_ref[...] = jnp.zeros_like(state_ref)
    groups = nch // unroll
    for _ in prep_stages(0):
        pass

    def pipelined_body(i, carry):
        run_interleaved(prep_stages(i), scan_steps(i - 1), every=scan_every)
        return carry

    lax.fori_loop(1, groups, pipelined_body, 0)
    for _ in scan_steps(groups - 1):
        pass


def _deltanet(proj_a, conv_w_t, gcol, grow, norm_w, batch, seq, unroll=8, scan_every=1):
    dk = HEAD_DIM
    nq = DN_QK_HEADS
    c = DN_CHUNK
    nch = seq // c
    return pl.pallas_call(
        functools.partial(_deltanet_kernel, seq=seq, unroll=unroll, scan_every=scan_every),
        grid=(batch, nq),
        in_specs=[pl.BlockSpec((seq, dk), lambda b, h: (b, h)),
                  pl.BlockSpec((seq, dk), lambda b, h: (b, nq + h)),
                  pl.BlockSpec((seq, 2 * dk), lambda b, h: (b, nq + h)),
                  pl.BlockSpec((seq, 2 * dk), lambda b, h: (b, 2 * nq + h)),
                  pl.BlockSpec((DN_CONV, dk), lambda b, h: (0, h)),
                  pl.BlockSpec((DN_CONV, dk), lambda b, h: (0, nq + h)),
                  pl.BlockSpec((DN_CONV, 2 * dk), lambda b, h: (0, nq + h)),
                  pl.BlockSpec((None, None, seq, 4), lambda b, h: (b, h, 0, 0)),
                  pl.BlockSpec((None, nch, DN_V_HEADS, c), lambda b, h: (b, 0, 0, 0)),
                  pl.BlockSpec((1, dk), lambda b, h: (0, 0))],
        out_specs=pl.BlockSpec((seq, 2 * dk), lambda b, h: (b, h)),
        out_shape=jax.ShapeDtypeStruct((batch * seq, A_V), BF16),
        scratch_shapes=[pltpu.VMEM((2, dk, dk), F32),
                        pltpu.VMEM((nch, 2, 2 * c, dk), BF16),
                        pltpu.VMEM((nch, 2, dk, dk), F32),
                        pltpu.VMEM((nch, 2, c, dk), F32),
                        pltpu.VMEM((4, 8 + seq, dk), F32)],
        compiler_params=_params("arbitrary", "arbitrary"),
        name="deltanet",
    )(proj_a, proj_a, proj_a, proj_a, conv_w_t, conv_w_t, conv_w_t, gcol, grow, norm_w.reshape(1, dk))


def _rope_table_kernel(cos_ref, sin_ref):
    shape = cos_ref.shape
    pos = lax.broadcasted_iota(jnp.int32, shape, 0).astype(F32)
    lane = lax.broadcasted_iota(jnp.int32, shape, 1)
    idx = (lane % ROPE_HALF).astype(F32)
    inv = jnp.exp(idx * (-2.0 / ROPE_DIM) * jnp.log(jnp.float32(ROPE_THETA)))
    ang = pos * inv
    cos = jnp.cos(ang)
    sin = jnp.sin(ang)
    cos_ref[...] = jnp.where(lane < ROPE_DIM, cos, 1.0)
    sin_ref[...] = jnp.where(lane < ROPE_HALF, -sin, jnp.where(lane < ROPE_DIM, sin, 0.0))


def _rope_tables(seq):
    shp = jax.ShapeDtypeStruct((seq, HEAD_DIM), F32)
    return pl.pallas_call(_rope_table_kernel, out_shape=[shp, shp], name="rope_tables")()


def _attn_kernel(q0_ref, q1_ref, q2_ref, k_ref, v_ref, z_ref, cos_ref, sin_ref, o_ref,
                 qs_ref, ks_ref, vs_ref, m_ref, l_ref, acc_ref, bias_ref, *, seq, unroll):
    wb = SW_BLOCK
    dh = HEAD_DIM
    rows = 1024
    pr = lax.broadcasted_iota(jnp.int32, (dh, dh), 0)
    pc = lax.broadcasted_iota(jnp.int32, (dh, dh), 1)
    swap = jnp.where(((pc < ROPE_HALF) & (pr == pc + ROPE_HALF))
                     | ((pc >= ROPE_HALF) & (pc < ROPE_DIM) & (pr == pc - ROPE_HALF)), 1.0, 0.0).astype(BF16)

    def rope(x, cos, sin):
        swapped = jnp.dot(x.astype(BF16), swap, preferred_element_type=F32)
        return x * cos + swapped * sin

    def rope_body(i, carry):
        r0 = pl.multiple_of(i * rows, rows)
        cos = cos_ref[pl.ds(r0, rows), :]
        sin = sin_ref[pl.ds(r0, rows), :]
        for g, q_ref in enumerate((q0_ref, q1_ref, q2_ref)):
            qs_ref[g, pl.ds(r0, rows), :] = rope(q_ref[pl.ds(r0, rows), :].astype(F32), cos, sin) * (dh ** -0.5)
        ks_ref[pl.ds(r0, rows), :] = rope(k_ref[pl.ds(r0, rows), :].astype(F32), cos, sin)
        vs_ref[pl.ds(r0, rows), :] = v_ref[pl.ds(r0, rows), :].astype(F32)
        return carry

    lax.fori_loop(0, seq // rows, rope_body, 0)

    delta = lax.broadcasted_iota(jnp.int32, (wb, 2 * wb), 1) - lax.broadcasted_iota(jnp.int32, (wb, 2 * wb), 0)
    bias_ref[0] = jnp.where((delta >= 0) & (delta <= wb), 0.0, NEG_BIG)
    bias_ref[1] = jnp.where(delta <= 0, 0.0, NEG_BIG)

    order = sorted(range(len(SW_DILATIONS)), key=lambda gi: -SW_DILATIONS[gi])
    for pos_in_order, g in enumerate(order):
        dil = SW_DILATIONS[g]
        first = pos_in_order == 0
        last = pos_in_order == len(order) - 1
        nb = max(seq // dil // wb, 1)
        has_prev = nb > 1
        kw = 2 * wb if has_prev else wb
        ones = jnp.ones((kw, dh), BF16)

        def block_body(it, carry, g=g, dil=dil, nb=nb, has_prev=has_prev, kw=kw, ones=ones,
                       first=first, last=last):
            def rows_of(ref, s, count):
                if dil == 1:
                    return ref[pl.ds(s, count), :]
                return ref[pl.ds(s, count, stride=dil), :]

            idxs = [it * unroll + u for u in range(unroll)]
            ns = [idx & (nb - 1) for idx in idxs]
            starts = [lax.shift_right_logical(idx, nb.bit_length() - 1) + n * (dil * wb) for idx, n in zip(idxs, ns)]
            if dil == 1:
                starts = [pl.multiple_of(s, wb) for s in starts]
            if has_prev:
                kstarts = [jnp.where(n > 0, s - dil * wb, s) for s, n in zip(starts, ns)]
                biases = [bias_ref[jnp.where(n > 0, 0, 1)] for n in ns]
            else:
                kstarts = starts
                biases = [bias_ref[1, :, 0:wb] for _ in ns]
            qs = [rows_of(qs_ref.at[g], s, wb).astype(BF16) for s in starts]
            kwins = [rows_of(ks_ref, s, kw).astype(BF16) for s in kstarts]
            vaugs = [jnp.concatenate([rows_of(vs_ref, s, kw).astype(BF16), ones], axis=1) for s in kstarts]
            scs = [lax.dot_general(q, kwin, _NT, preferred_element_type=F32) + bias
                   for bias, q, kwin in zip(biases, qs, kwins)]
            m_blks = [jnp.max(sc, axis=-1, keepdims=True) for sc in scs]
            ps = [jnp.exp(sc - m).astype(BF16) for sc, m in zip(scs, m_blks)]
            pvs = [jnp.dot(p, vaug, preferred_element_type=F32) for p, vaug in zip(ps, vaugs)]
            for s, m_blk, pv in zip(starts, m_blks, pvs):
                m_blk = jnp.broadcast_to(m_blk, (wb, dh))
                acc_blk = pv[:, :dh]
                l_blk = pv[:, dh:]
                sl = pl.ds(s, wb) if dil == 1 else pl.ds(s, wb, stride=dil)
                if first:
                    m_ref[sl, :] = m_blk
                    l_ref[sl, :] = l_blk
                    acc_ref[sl, :] = acc_blk
                    continue
                m_old = m_ref[sl, :]
                m_new = jnp.maximum(m_old, m_blk)
                a_old = jnp.exp(m_old - m_new)
                a_blk = jnp.exp(m_blk - m_new)
                l_new = a_old * l_ref[sl, :] + a_blk * l_blk
                acc_new = a_old * acc_ref[sl, :] + a_blk * acc_blk
                if last:
                    z = z_ref[sl, :].astype(F32)
                    o_ref[sl, :] = (acc_new / l_new * _silu(z)).astype(o_ref.dtype)
                else:
                    m_ref[sl, :] = m_new
                    l_ref[sl, :] = l_new
                    acc_ref[sl, :] = acc_new
            return carry

        lax.fori_loop(0, dil * nb // unroll, block_body, 0)


def _dilated_attention(proj_b, cos_tab, sin_tab, batch, seq, unroll=16):
    dh = HEAD_DIM
    nh = SW_HEADS
    ng = len(SW_DILATIONS)

    def col(block):
        return pl.BlockSpec((seq, dh), lambda b, h: (b, A_WIDTH // dh + block * nh + h))

    tab = pl.BlockSpec((seq, dh), lambda b, h: (0, 0))
    return pl.pallas_call(
        functools.partial(_attn_kernel, seq=seq, unroll=unroll),
        grid=(batch, nh),
        in_specs=[col(0), col(1), col(2), col(ng), col(ng + 1), col(ng + 2), tab, tab],
        out_specs=pl.BlockSpec((seq, dh), lambda b, h: (b, h)),
        out_shape=jax.ShapeDtypeStruct((batch * seq, nh * dh), BF16),
        scratch_shapes=[pltpu.VMEM((ng, seq, dh), F32), pltpu.VMEM((seq, dh), F32), pltpu.VMEM((seq, dh), F32),
                        pltpu.VMEM((seq, dh), F32), pltpu.VMEM((seq, dh), F32), pltpu.VMEM((seq, dh), F32),
                        pltpu.VMEM((2, SW_BLOCK, 2 * SW_BLOCK), F32)],
        compiler_params=_params("arbitrary", "arbitrary"),
        name="dilated_attn",
    )(proj_b, proj_b, proj_b, proj_b, proj_b, proj_b, cos_tab, sin_tab)


def _shortconv_kernel(x_ref, wb_ref, wc_ref, wu_ref, wz_ref, cw_ref, cast_src_ref, o_ref, cast_out_ref, wq_ref, halo_ref,
                      *, steps_per_seq):
    i = pl.program_id(1)
    cast_out_ref[...] = cast_src_ref[...].astype(BF16)

    @pl.when(i == 0)
    def _():
        for g, w_ref in enumerate((wb_ref, wc_ref, wu_ref, wz_ref)):
            wq_ref[g] = w_ref[...].astype(BF16)

    x = x_ref[...]
    tm = x.shape[0]
    gate_c = jnp.dot(x, wq_ref[1], preferred_element_type=F32)
    u = jnp.dot(x, wq_ref[2], preferred_element_type=F32)
    cu = gate_c * u
    prev = jnp.where(i % steps_per_seq != 0, halo_ref[...], 0.0)
    halo_ref[...] = cu[tm - 8:, :]
    xx = jnp.concatenate([prev, cu], axis=0)
    conv = cw_ref[SC_CONV - 1:SC_CONV, :] * cu
    for t in range(SC_CONV - 1):
        lo = 8 - (SC_CONV - 1) + t
        conv = conv + cw_ref[t:t + 1, :] * xx[lo:lo + tm, :]
    gate_b = jnp.dot(x, wq_ref[0], preferred_element_type=F32)
    z = jnp.dot(x, wq_ref[3], preferred_element_type=F32)
    o_ref[...] = (gate_b * conv * _silu(z)).astype(o_ref.dtype)


def _shortconv_mix(hn, w_in, layer, conv_w_t, w_out, seq, tm=1024, tn=256):
    m, k = hn.shape
    nblk = SC_WIDTH // tn
    n_i = m // tm
    _, r, c = w_out.shape
    rows = r // (nblk * n_i)
    assert rows * nblk * n_i == r

    def wspec(g):
        return pl.BlockSpec((None, k, tn), lambda j, i: (layer, 0, g * nblk + j))

    return pl.pallas_call(
        functools.partial(_shortconv_kernel, steps_per_seq=seq // tm),
        grid=(nblk, n_i),
        in_specs=[pl.BlockSpec((tm, k), lambda j, i: (i, 0)), wspec(0), wspec(1), wspec(2), wspec(3),
                  pl.BlockSpec((None, SC_CONV, tn), lambda j, i: (layer, 0, j)),
                  pl.BlockSpec((None, rows, c), lambda j, i: (layer, j * n_i + i, 0))],
        out_specs=[pl.BlockSpec((tm, tn), lambda j, i: (i, j)),
                   pl.BlockSpec((rows, c), lambda j, i: (j * n_i + i, 0))],
        out_shape=[jax.ShapeDtypeStruct((m, SC_WIDTH), BF16), jax.ShapeDtypeStruct((r, c), BF16)],
        scratch_shapes=[pltpu.VMEM((4, k, tn), BF16), pltpu.VMEM((8, tn), F32)],
        compiler_params=_params("arbitrary", "arbitrary"),
        name="shortconv_mix",
    )(hn, w_in, w_in, w_in, w_in, conv_w_t, w_out)


def kernel(x, norm_w, hyb_w_in, dn_conv_w, dn_a_log, dn_dt_bias, dn_norm_w, hyb_w_out, sc_w_in, sc_conv_w,
           sc_w_out, final_norm_w):
    batch, seq, d = x.shape
    h = x.reshape(batch * seq, d)
    cos_tab, sin_tab = _rope_tables(seq)
    dn_conv_t = jnp.swapaxes(dn_conv_w, 1, 2)
    sc_conv_t = jnp.swapaxes(sc_conv_w, 1, 2)
    hyb_wt = jnp.swapaxes(hyb_w_in, 1, 2)
    hn = None
    for layer in range(DEPTH):
        li = layer // 2
        last = layer == DEPTH - 1
        next_norm = final_norm_w if last else norm_w[layer + 1]
        if layer % 2 == 0:
            if hn is None:
                gcol, grow, hn = _gates(h, hyb_wt, li, dn_a_log[li], dn_dt_bias[li], batch, seq, norm_w=norm_w[layer])
            else:
                gcol, grow = _gates(hn, hyb_wt, li, dn_a_log[li], dn_dt_bias[li], batch, seq)
            proj, w_out = _matmul_nt(hn, hyb_wt, li, A_WIDTH + B_WIDTH, GATE_OFF, B_OFF - GATE_OFF, BF16,
                                     cast_src=hyb_w_out)
            y_a = _deltanet(proj, dn_conv_t[li], gcol, grow, dn_norm_w[li], batch, seq)
            y_b = _dilated_attention(proj, cos_tab, sin_tab, batch, seq)
            outs = _out_proj_residual_norm([y_a, y_b], w_out, h, next_norm, last)
        else:
            y, w_out = _shortconv_mix(hn, sc_w_in, li, sc_conv_t, sc_w_out, seq)
            outs = _out_proj_residual_norm([y], w_out, h, next_norm, last)
        if last:
            return outs[0].reshape(batch, seq, d)
        h, hn = outs
```
